```python
import math
import jax, jax.numpy as jnp
from jax import lax
import numpy as np

D_MODEL = 1024
BATCH = 2
SEQ = 8192
DEPTH = 1
DEC_BATCH = 1
DEC_SEQ = 16384
PAST_LEN = 128

HEAD_DIM = 64
A_Q_HEADS = 8
A_KV_HEADS = 2
A_GROUP = A_Q_HEADS // A_KV_HEADS
A_WIDTH = A_Q_HEADS * HEAD_DIM
B_HEADS = 4
B_V_DIM = 2 * HEAD_DIM
B_WIDTH = B_HEADS * B_V_DIM
MIX_WIDTH = A_WIDTH + B_WIDTH
A_Q_COLS = A_Q_HEADS * HEAD_DIM
A_KV_COLS = A_KV_HEADS * HEAD_DIM
B_QK_COLS = B_HEADS * 2 * HEAD_DIM
B_V_COLS = B_HEADS * B_V_DIM
IN_COLS = A_Q_COLS + 2 * A_KV_COLS + 2 * B_QK_COLS + B_V_COLS
GRID_W = 64
ROPE_THETA = 10000.0
REL_BUCKETS = 32
REL_MAX_DIST = 128
Q_BLOCK = 128
N_EXPERTS = 16
CAPACITY_FACTOR = 2
EXPERT_FF = 2816
NORM_EPS = 1e-6
LAMBDA_INIT_A = 0.8
LAMBDA_INIT_B = 0.6
LAMBDA_INIT_C = 0.3

kernel_name = "hymba_gqa_axial_diffattn_expert_choice_encoder"


def rmsnorm(x, g):
    xf = x.astype(jnp.float32)
    y = xf * lax.rsqrt(jnp.mean(xf * xf, axis=-1, keepdims=True) + NORM_EPS)
    return (y * g.astype(jnp.float32)).astype(x.dtype)


def axial_rope_tables(n):
    rows = n // GRID_W
    row = jnp.broadcast_to(jnp.arange(rows)[:, None], (rows, GRID_W)).reshape(-1).astype(jnp.float32)
    col = jnp.broadcast_to(jnp.arange(GRID_W)[None, :], (rows, GRID_W)).reshape(-1).astype(jnp.float32)
    half = HEAD_DIM // 2
    inv = ROPE_THETA ** (-jnp.arange(0, half, 2, dtype=jnp.float32) / half)
    ang_r = row[:, None] * inv[None, :]
    ang_c = col[:, None] * inv[None, :]
    cos = jnp.concatenate([jnp.cos(ang_r), jnp.cos(ang_r), jnp.cos(ang_c), jnp.cos(ang_c)], axis=-1)
    sin = jnp.concatenate([jnp.sin(ang_r), jnp.sin(ang_r), jnp.sin(ang_c), jnp.sin(ang_c)], axis=-1)
    return cos, sin


def apply_axial_rope(x, cos, sin):
    a, b, c, d = jnp.split(x.astype(jnp.float32), 4, axis=-1)
    rot = jnp.concatenate([-b, a, -d, c], axis=-1)
    y = x.astype(jnp.float32) * cos[None, :, None, :] + rot * sin[None, :, None, :]
    return y.astype(x.dtype)


def t5_bucket(rel):
    nb = REL_BUCKETS // 2
    ret = (rel > 0).astype(jnp.int32) * nb
    n = jnp.abs(rel)
    max_exact = nb // 2
    nf = jnp.maximum(n, 1).astype(jnp.float32)
    large = max_exact + (jnp.log(nf / max_exact) / math.log(REL_MAX_DIST / max_exact)
                         * (nb - max_exact)).astype(jnp.int32)
    large = jnp.minimum(large, nb - 1)
    return ret + jnp.where(n < max_exact, n, large)


def gqa_axial_attention(q, k, v, q_gain, k_gain):
    bsz, n = q.shape[0], q.shape[1]
    cos, sin = axial_rope_tables(n)
    q = apply_axial_rope(rmsnorm(q, q_gain), cos, sin)
    k = apply_axial_rope(rmsnorm(k, k_gain), cos, sin)
    qg = q.reshape(bsz, n, A_KV_HEADS, A_GROUP, HEAD_DIM)
    scale = HEAD_DIM ** -0.5

    def block(i):
        qb = lax.dynamic_slice_in_dim(qg, i * Q_BLOCK, Q_BLOCK, axis=1)
        s = jnp.einsum('bqkgd,bskd->bkgqs', qb, k).astype(jnp.float32) * scale
        p = jax.nn.softmax(s, axis=-1).astype(v.dtype)
        return jnp.einsum('bkgqs,bskd->bqkgd', p, v)

    out = lax.map(block, jnp.arange(n // Q_BLOCK))
    return jnp.moveaxis(out, 0, 1).reshape(bsz, n, A_WIDTH)


def diff_attention(q, k, v, lam_params, subln_gain, rel_bias, lambda_init):
    bsz, n = q.shape[0], q.shape[1]
    lp = lam_params.astype(jnp.float32)
    lam = jnp.exp(jnp.sum(lp[0] * lp[1])) - jnp.exp(jnp.sum(lp[2] * lp[3])) + lambda_init
    scale = HEAD_DIM ** -0.5
    kpos = jnp.arange(n)
    table = rel_bias.astype(jnp.float32)

    def block(i):
        qb = lax.dynamic_slice_in_dim(q, i * Q_BLOCK, Q_BLOCK, axis=1)
        s = jnp.einsum('bqhcd,bshcd->bchqs', qb, k).astype(jnp.float32) * scale
        qpos = i * Q_BLOCK + jnp.arange(Q_BLOCK)
        bucket = t5_bucket(kpos[None, :] - qpos[:, None])
        bias = jnp.moveaxis(table[bucket], -1, 0)
        p = jax.nn.softmax(s + bias[None, None], axis=-1)
        w = (p[:, 0] - lam * p[:, 1]).astype(v.dtype)
        return jnp.einsum('bhqs,bshe->bqhe', w, v)

    out = lax.map(block, jnp.arange(n // Q_BLOCK))
    out = jnp.moveaxis(out, 0, 1).reshape(bsz, n, B_HEADS, B_V_DIM)
    out = rmsnorm(out, subln_gain) * (1.0 - lambda_init)
    return out.reshape(bsz, n, B_WIDTH).astype(v.dtype)


def expert_choice_ffn(h, w_router, w_gate, w_up, w_down):
    bsz, n, d = h.shape
    tok = h.reshape(bsz * n, d)
    n_tok = bsz * n
    cap = CAPACITY_FACTOR * n_tok // N_EXPERTS
    aff = jax.nn.softmax(jnp.matmul(tok, w_router).astype(jnp.float32), axis=-1)
    g, idx = lax.top_k(aff.T, cap)
    xe = tok[idx]
    hid = jax.nn.silu(jnp.einsum('ecd,edf->ecf', xe, w_gate)) * jnp.einsum('ecd,edf->ecf', xe, w_up)
    ye = jnp.einsum('ecf,efd->ecd', hid, w_down) * g[..., None].astype(h.dtype)
    out = jnp.zeros_like(tok).at[idx.reshape(-1)].add(ye.reshape(-1, d))
    return out.reshape(bsz, n, d)


def encoder_forward(x, norm_mix, w_in, a_q_norm, a_k_norm, b_lambda, b_subln, w_out,
                    rel_bias, norm_ffn, w_router, w_gate, w_up, w_down, norm_final):
    bsz, n, _ = x.shape
    splits = np.cumsum([A_Q_COLS, A_KV_COLS, A_KV_COLS, B_QK_COLS, B_QK_COLS]).tolist()
    for l in range(DEPTH):
        lambda_init = LAMBDA_INIT_A - LAMBDA_INIT_B * math.exp(-LAMBDA_INIT_C * l)
        h = rmsnorm(x, norm_mix[l])
        proj = jnp.matmul(h, w_in[l])
        aq, ak, av, bq, bk, bv = jnp.split(proj, splits, axis=-1)
        a_out = gqa_axial_attention(
            aq.reshape(bsz, n, A_Q_HEADS, HEAD_DIM),
            ak.reshape(bsz, n, A_KV_HEADS, HEAD_DIM),
            av.reshape(bsz, n, A_KV_HEADS, HEAD_DIM),
            a_q_norm[l], a_k_norm[l])
        b_out = diff_attention(
            bq.reshape(bsz, n, B_HEADS, 2, HEAD_DIM),
            bk.reshape(bsz, n, B_HEADS, 2, HEAD_DIM),
            bv.reshape(bsz, n, B_HEADS, B_V_DIM),
            b_lambda[l], b_subln[l], rel_bias, lambda_init)
        x = x + jnp.matmul(jnp.concatenate([a_out, b_out], axis=-1), w_out[l])
        x = x + expert_choice_ffn(rmsnorm(x, norm_ffn[l]), w_router[l], w_gate[l], w_up[l], w_down[l])
    return rmsnorm(x, norm_final)


def setup_inputs(seed: int = 0) -> dict:
    key = jax.random.key(seed)
    ks = jax.random.split(key, 20)
    f32 = jnp.float32
    nrm = lambda k, shape, s: (jax.random.normal(k, shape, f32) * s).astype(f32)
    return {
        "x_prompt": nrm(ks[0], (BATCH, SEQ, D_MODEL), 1.0),
        "x_sample": nrm(ks[1], (DEC_BATCH, DEC_SEQ, D_MODEL), 1.0),
        "norm_mix": 1.0 + nrm(ks[2], (DEPTH, D_MODEL), 0.02),
        "w_in": nrm(ks[3], (DEPTH, D_MODEL, IN_COLS), D_MODEL ** -0.5),
        "a_q_norm": 1.0 + nrm(ks[4], (DEPTH, HEAD_DIM), 0.02),
        "a_k_norm": 1.0 + nrm(ks[5], (DEPTH, HEAD_DIM), 0.02),
        "b_lambda": nrm(ks[6], (DEPTH, 4, HEAD_DIM), 0.1),
        "b_subln": 1.0 + nrm(ks[7], (DEPTH, B_V_DIM), 0.02),
        "w_out": nrm(ks[8], (DEPTH, MIX_WIDTH, D_MODEL), MIX_WIDTH ** -0.5),
        "rel_bias": nrm(ks[9], (REL_BUCKETS, B_HEADS), 0.5),
        "norm_ffn": 1.0 + nrm(ks[10], (DEPTH, D_MODEL), 0.02),
        "w_router": nrm(ks[11], (DEPTH, D_MODEL, N_EXPERTS), D_MODEL ** -0.5),
        "w_gate": nrm(ks[12], (DEPTH, N_EXPERTS, D_MODEL, EXPERT_FF), D_MODEL ** -0.5),
        "w_up": nrm(ks[13], (DEPTH, N_EXPERTS, D_MODEL, EXPERT_FF), D_MODEL ** -0.5),
        "w_down": nrm(ks[14], (DEPTH, N_EXPERTS, EXPERT_FF, D_MODEL), EXPERT_FF ** -0.5),
        "norm_final": 1.0 + nrm(ks[15], (D_MODEL,), 0.02),
    }


def reference(x_prompt, x_sample, norm_mix, w_in, a_q_norm, a_k_norm, b_lambda, b_subln, w_out,
              rel_bias, norm_ffn, w_router, w_gate, w_up, w_down, norm_final):
    y_prompt = encoder_forward(x_prompt, norm_mix, w_in, a_q_norm, a_k_norm, b_lambda, b_subln, w_out,
                               rel_bias, norm_ffn, w_router, w_gate, w_up, w_down, norm_final)
    y_sample = encoder_forward(x_sample, norm_mix, w_in, a_q_norm, a_k_norm, b_lambda, b_subln, w_out,
                               rel_bias, norm_ffn, w_router, w_gate, w_up, w_down, norm_final)
    return (y_prompt, y_sample)
```

```python
import functools
import math

import jax
import jax.numpy as jnp
import numpy as np
from jax import lax
from jax.experimental import pallas as pl
from jax.experimental.pallas import tpu as pltpu

F32 = jnp.float32
BF16 = jnp.bfloat16
I32 = jnp.int32

HEAD_DIM = 64
A_Q_HEADS = 8
A_KV_HEADS = 2
B_HEADS = 4
A_Q_COLS = A_Q_HEADS * HEAD_DIM
A_KV_COLS = A_KV_HEADS * HEAD_DIM
B_QK_COLS = B_HEADS * 2 * HEAD_DIM
B_V_COLS = B_HEADS * 2 * HEAD_DIM
GRID_W = 64
ROPE_THETA = 10000.0
REL_BUCKETS = 32
REL_MAX_DIST = 128
CAPACITY_FACTOR = 2
NORM_EPS = 1e-6
LAMBDA_INIT = 0.8 - 0.6 * math.exp(-0.3 * 0)
QK_SCALE = HEAD_DIM ** -0.5

LANES = 128
VMEM_LIMIT_BYTES = 56 * 1024 * 1024

TOKEN_TILE = 512
ATTN_TQ = 256
ATTN_TK = 512
SUB = 128
GATHER_WIN = SUB + 16
GATHER_TB = 2048
COMBINE_TB = 512
COMBINE_WIN = 256
FFN_TF = 256
NEG_BIG = -1e30


def _dot(a, b):
    return jnp.dot(a, b, preferred_element_type=F32)


def _cparams(sem):
    return pltpu.CompilerParams(dimension_semantics=sem, vmem_limit_bytes=VMEM_LIMIT_BYTES)


def _split_dot(v, w_bf16):
    hi = v.astype(BF16)
    lo = (v - hi.astype(F32)).astype(BF16)
    return _dot(hi, w_bf16) + _dot(lo, w_bf16)


def _proj_kernel(x_ref, gmix_ref, w_ref, gq_ref, gk_ref, cos_ref, sin_ref, seg_ref,
                 qa_ref, kat_ref, va_ref, qb_ref, kbt_ref, vb_ref):
    x = x_ref[...]
    h = (x * lax.rsqrt(jnp.mean(x * x, axis=-1, keepdims=True) + NORM_EPS) * gmix_ref[...]).astype(BF16)
    cos = cos_ref[...]
    sin = sin_ref[...]
    seg = seg_ref[...]
    lane = lax.broadcasted_iota(I32, (1, LANES), 1)
    first_half = (lane % 32) < 16
    low_head = lane < HEAD_DIM

    def norm_rope(y, gain):
        ss = _split_dot(y * y, seg)
        y = y * lax.rsqrt(ss * (1.0 / HEAD_DIM) + NORM_EPS) * gain
        rot = jnp.where(first_half, pltpu.roll(y, LANES - 16, 1), pltpu.roll(y, 16, 1))
        return y * cos + rot * sin

    c0 = 0
    aq = _dot(h, w_ref[:, c0:c0 + A_Q_COLS])
    for c in range(A_Q_COLS // LANES):
        y = norm_rope(aq[:, c * LANES:(c + 1) * LANES], gq_ref[...])
        qa_ref[:, c * LANES:(c + 1) * LANES] = (y * QK_SCALE).astype(BF16)
    c0 += A_Q_COLS

    akv = _dot(h, w_ref[:, c0:c0 + 2 * A_KV_COLS])
    kt = norm_rope(akv[:, :LANES], gk_ref[...]).T
    for g in range(A_KV_HEADS):
        kg = kt[g * HEAD_DIM:(g + 1) * HEAD_DIM]
        kat_ref[g] = jnp.concatenate([kg, kg], axis=0).astype(BF16)
    v = akv[:, LANES:]
    v_sw = pltpu.roll(v, HEAD_DIM, 1)
    va_ref[0] = jnp.where(low_head, v, v_sw).astype(BF16)
    va_ref[1] = jnp.where(low_head, v_sw, v).astype(BF16)
    c0 += 2 * A_KV_COLS

    qb_ref[...] = (_dot(h, w_ref[:, c0:c0 + B_QK_COLS]) * QK_SCALE).astype(BF16)
    c0 += B_QK_COLS
    bk = _dot(h, w_ref[:, c0:c0 + B_QK_COLS])
    for hd in range(B_HEADS):
        kbt_ref[hd] = bk[:, hd * LANES:(hd + 1) * LANES].T.astype(BF16)
    c0 += B_QK_COLS
    vb_ref[...] = _dot(h, w_ref[:, c0:c0 + B_V_COLS]).astype(BF16)


def _proj(x, gmix, w_in, gq, gk, cos, sin, seg, n):
    ntok, d = x.shape
    tm = min(TOKEN_TILE, n)
    per_seq = n // tm
    tok = lambda i: (i, 0)
    full = lambda i: (0, 0)
    return pl.pallas_call(
        _proj_kernel,
        grid=(ntok // tm,),
        in_specs=[
            pl.BlockSpec((tm, d), tok),
            pl.BlockSpec((1, d), full),
            pl.BlockSpec(w_in.shape, full),
            pl.BlockSpec((1, LANES), full),
            pl.BlockSpec((1, LANES), full),
            pl.BlockSpec((tm, LANES), lambda i: (i % per_seq, 0)),
            pl.BlockSpec((tm, LANES), lambda i: (i % per_seq, 0)),
            pl.BlockSpec((LANES, LANES), full),
        ],
        out_specs=[
            pl.BlockSpec((tm, A_Q_COLS), tok),
            pl.BlockSpec((A_KV_HEADS, LANES, tm), lambda i: (0, 0, i)),
            pl.BlockSpec((A_KV_HEADS, tm, LANES), lambda i: (0, i, 0)),
            pl.BlockSpec((tm, B_QK_COLS), tok),
            pl.BlockSpec((B_HEADS, LANES, tm), lambda i: (0, 0, i)),
            pl.BlockSpec((tm, B_V_COLS), tok),
        ],
        out_shape=[
            jax.ShapeDtypeStruct((ntok, A_Q_COLS), BF16),
            jax.ShapeDtypeStruct((A_KV_HEADS, LANES, ntok), BF16),
            jax.ShapeDtypeStruct((A_KV_HEADS, ntok, LANES), BF16),
            jax.ShapeDtypeStruct((ntok, B_QK_COLS), BF16),
            jax.ShapeDtypeStruct((B_HEADS, LANES, ntok), BF16),
            jax.ShapeDtypeStruct((ntok, B_V_COLS), BF16),
        ],
        compiler_params=_cparams(("arbitrary",)),
        name="proj",
    )(x, gmix, w_in, gq, gk, cos, sin, seg)


def _flash_step(q, kt, v, state, bias=None):
    m, l, acc = state
    s = _dot(q, kt)
    if bias is not None:
        s = s + bias
    m_new = jnp.maximum(m, jnp.max(s, axis=-1, keepdims=True))
    p = jnp.exp(s - m_new)
    alpha = jnp.exp(m - m_new)
    l = alpha * l + jnp.sum(p, axis=-1, keepdims=True)
    acc = alpha * acc + _dot(p.astype(BF16), v)
    return m_new, l, acc


def _flash_init(tq):
    return (jnp.full((tq, 1), NEG_BIG, F32), jnp.zeros((tq, 1), F32), jnp.zeros((tq, LANES), F32))


def _attn_a_kernel(q_ref, kt_ref, v_ref, o_ref, *, tk):
    tq = q_ref.shape[0]
    nk = kt_ref.shape[-1] // tk
    low_head = lax.broadcasted_iota(I32, (1, LANES), 1) < HEAD_DIM
    n_pairs = q_ref.shape[1] // LANES
    qs = []
    for pr in range(n_pairs):
        qp = q_ref[:, pr * LANES:(pr + 1) * LANES]
        zero = jnp.zeros_like(qp)
        qs.append(jnp.where(low_head, qp, zero))
        qs.append(jnp.where(low_head, zero, qp))

    def body(j, carry):
        off = pl.multiple_of(j * tk, tk)
        kt = kt_ref[0, :, pl.ds(off, tk)]
        v = v_ref[0, pl.ds(off, tk), :]
        return tuple(_flash_step(qs[hd], kt, v, carry[hd]) for hd in range(len(qs)))

    res = lax.fori_loop(0, nk, body, tuple(_flash_init(tq) for _ in qs))
    for pr in range(n_pairs):
        even = res[2 * pr][2] / res[2 * pr][1]
        odd = res[2 * pr + 1][2] / res[2 * pr + 1][1]
        o_ref[:, pr * LANES:(pr + 1) * LANES] = jnp.where(low_head, even, odd).astype(BF16)


def _attn_a(qa, kat, va, bsz, n):
    ntok = qa.shape[0]
    tq = min(ATTN_TQ, n)
    tk = min(ATTN_TK, n)
    nq = n // tq
    gcols = A_Q_COLS // A_KV_HEADS
    return pl.pallas_call(
        functools.partial(_attn_a_kernel, tk=tk),
        grid=(bsz, A_KV_HEADS, nq),
        in_specs=[
            pl.BlockSpec((tq, gcols), lambda b, g, i: (b * nq + i, g)),
            pl.BlockSpec((1, LANES, n), lambda b, g, i: (g, 0, b)),
            pl.BlockSpec((1, n, LANES), lambda b, g, i: (g, b, 0)),
        ],
        out_specs=pl.BlockSpec((tq, gcols), lambda b, g, i: (b * nq + i, g)),
        out_shape=jax.ShapeDtypeStruct((ntok, A_Q_COLS), BF16),
        compiler_params=_cparams(("arbitrary", "arbitrary", "arbitrary")),
        name="attn_a",
    )(qa, kat, va)


def _attn_b_kernel(q_ref, kt_ref, v_ref, bias_ref, lam_ref, gsub_ref, o_ref, *, tk, d0, dstep):
    tq = q_ref.shape[0]
    nk = kt_ref.shape[-1] // tk
    n_tiles = bias_ref.shape[1]
    qi = pl.program_id(2)
    low_head = lax.broadcasted_iota(I32, (1, LANES), 1) < HEAD_DIM
    qp = q_ref[...]
    zero = jnp.zeros_like(qp)
    q1 = jnp.where(low_head, qp, zero)
    q2 = jnp.where(low_head, zero, qp)

    def body(j, carry):
        off = pl.multiple_of(j * tk, tk)
        kt = kt_ref[0, :, pl.ds(off, tk)]
        v = v_ref[pl.ds(off, tk), :]
        u = jnp.clip(j * tk - qi * tq - d0 + dstep, 0, (n_tiles - 1) * dstep)
        bias = bias_ref[0, lax.div(u, dstep)]
        return (_flash_step(q1, kt, v, carry[0], bias), _flash_step(q2, kt, v, carry[1], bias))

    (m1, l1, a1), (m2, l2, a2) = lax.fori_loop(0, nk, body, (_flash_init(tq), _flash_init(tq)))
    lp = lam_ref[...]
    lam = (jnp.exp(jnp.sum(lp[0:1] * lp[1:2], axis=-1, keepdims=True))
           - jnp.exp(jnp.sum(lp[2:3] * lp[3:4], axis=-1, keepdims=True)) + LAMBDA_INIT)
    out = a1 / l1 - lam * (a2 / l2)
    out = out * lax.rsqrt(jnp.mean(out * out, axis=-1, keepdims=True) + NORM_EPS) * gsub_ref[...]
    o_ref[...] = (out * (1.0 - LAMBDA_INIT)).astype(BF16)


def _t5_bucket(rel):
    nb = REL_BUCKETS // 2
    ret = (rel > 0).astype(I32) * nb
    n = jnp.abs(rel)
    max_exact = nb // 2
    nf = jnp.maximum(n, 1).astype(F32)
    large = max_exact + (jnp.log(nf / max_exact) / math.log(REL_MAX_DIST / max_exact)
                         * (nb - max_exact)).astype(I32)
    large = jnp.minimum(large, nb - 1)
    return ret + jnp.where(n < max_exact, n, large)


def _bias_tiles(rel_bias, tq, tk):
    dstep = math.gcd(tq, tk)
    reach = (tq + tk + REL_MAX_DIST) // dstep
    band = [d for d in range(-reach * dstep, reach * dstep + 1, dstep)
            if d + tk - 1 > -REL_MAX_DIST and d - (tq - 1) < REL_MAX_DIST]
    far = tq + tk + REL_MAX_DIST
    ds = jnp.asarray([-far] + band + [far], I32)
    rel = (ds[:, None, None] + jnp.arange(tk, dtype=I32)[None, None, :]
           - jnp.arange(tq, dtype=I32)[None, :, None])
    tiles = rel_bias.astype(F32)[_t5_bucket(rel)]
    return jnp.moveaxis(tiles, -1, 0), band[0], dstep


def _attn_b(qb, kbt, vb, rel_bias, lam_params, gsub, bsz, n):
    ntok = qb.shape[0]
    tq = min(ATTN_TQ, n)
    tk = min(ATTN_TK, n)
    nq = n // tq
    tiles, d0, dstep = _bias_tiles(rel_bias, tq, tk)
    return pl.pallas_call(
        functools.partial(_attn_b_kernel, tk=tk, d0=d0, dstep=dstep),
        grid=(bsz, B_HEADS, nq),
        in_specs=[
            pl.BlockSpec((tq, LANES), lambda b, h, i: (b * nq + i, h)),
            pl.BlockSpec((1, LANES, n), lambda b, h, i: (h, 0, b)),
            pl.BlockSpec((n, LANES), lambda b, h, i: (b, h)),
            pl.BlockSpec((1,) + tiles.shape[1:], lambda b, h, i: (h, 0, 0, 0)),
            pl.BlockSpec(lam_params.shape, lambda b, h, i: (0, 0)),
            pl.BlockSpec((1, LANES), lambda b, h, i: (0, 0)),
        ],
        out_specs=pl.BlockSpec((tq, LANES), lambda b, h, i: (b * nq + i, h)),
        out_shape=jax.ShapeDtypeStruct((ntok, B_V_COLS), BF16),
        compiler_params=_cparams(("arbitrary", "arbitrary", "arbitrary")),
        name="attn_b",
    )(qb, kbt, vb, tiles, lam_params, gsub)


def _outproj_kernel(x_ref, a_ref, b_ref, wo_ref, gffn_ref, wr_hi_ref, wr_lo_ref,
                    x1_ref, h2_ref, afft_ref, *, n_experts):
    wa = wo_ref.shape[0] // 2
    x1 = x_ref[...] + _dot(a_ref[...], wo_ref[:wa]) + _dot(b_ref[...], wo_ref[wa:])
    x1_ref[...] = x1
    h2 = x1 * lax.rsqrt(jnp.mean(x1 * x1, axis=-1, keepdims=True) + NORM_EPS) * gffn_ref[...]
    h2_ref[...] = h2.astype(BF16)
    hi = h2.astype(BF16)
    lo = (h2 - hi.astype(F32)).astype(BF16)
    logits = _dot(hi, wr_hi_ref[...]) + _dot(hi, wr_lo_ref[...]) + _dot(lo, wr_hi_ref[...])
    lane = lax.broadcasted_iota(I32, (1, LANES), 1)
    logits = jnp.where(lane < n_experts, logits, NEG_BIG)
    ex = jnp.exp(logits - jnp.max(logits, axis=-1, keepdims=True))
    aff = ex / jnp.sum(ex, axis=-1, keepdims=True)
    afft_ref[...] = aff.T[:n_experts]


def _outproj(x, a_out, b_out, w_out, gffn, wr_hi, wr_lo, n_experts):
    ntok, d = x.shape
    tm = min(TOKEN_TILE, ntok)
    tok = lambda i: (i, 0)
    full = lambda i: (0, 0)
    return pl.pallas_call(
        functools.partial(_outproj_kernel, n_experts=n_experts),
        grid=(ntok // tm,),
        in_specs=[
            pl.BlockSpec((tm, d), tok),
            pl.BlockSpec((tm, a_out.shape[1]), tok),
            pl.BlockSpec((tm, b_out.shape[1]), tok),
            pl.BlockSpec(w_out.shape, full),
            pl.BlockSpec((1, d), full),
            pl.BlockSpec(wr_hi.shape, full),
            pl.BlockSpec(wr_lo.shape, full),
        ],
        out_specs=[
            pl.BlockSpec((tm, d), tok),
            pl.BlockSpec((tm, d), tok),
            pl.BlockSpec((n_experts, tm), lambda i: (0, i)),
        ],
        out_shape=[
            jax.ShapeDtypeStruct((ntok, d), F32),
            jax.ShapeDtypeStruct((ntok, d), BF16),
            jax.ShapeDtypeStruct((n_experts, ntok), F32),
        ],
        compiler_params=_cparams(("arbitrary",)),
        name="outproj",
    )(x, a_out, b_out, w_out, gffn, wr_hi, wr_lo)


def _topk_kernel(aff_ref, slot_ref, gsel_ref, cstart_ref, *, cap):
    aff = aff_ref[...]
    n_exp, ntok = aff.shape
    bits = lax.bitcast_convert_type(aff, I32)

    def count(mask):
        return jnp.sum(mask.astype(I32), axis=1, keepdims=True)

    def value_bit(i, t):
        cand = t | lax.shift_left(jnp.int32(1), 30 - i)
        return jnp.where(count(bits >= cand) >= cap, cand, t)

    thr = lax.fori_loop(0, 31, value_bit, jnp.zeros((n_exp, 1), I32))
    above = bits > thr
    tie = bits == thr
    need = cap - count(above)
    idx = lax.broadcasted_iota(I32, aff.shape, 1)

    n_bits = int(ntok).bit_length()

    def index_bit(i, bound):
        cand = bound + lax.shift_left(jnp.int32(1), n_bits - 1 - i)
        ok = (cand <= ntok) & (count(tie & (idx < cand)) <= need)
        return jnp.where(ok, cand, bound)

    bound = lax.fori_loop(0, n_bits, index_bit, jnp.zeros((n_exp, 1), I32))
    sel = above | (tie & (idx < bound))
    gsel_ref[...] = jnp.where(sel, aff, 0.0)

    r = lax.broadcasted_iota(I32, (SUB, SUB), 0)
    c = lax.broadcasted_iota(I32, (SUB, SUB), 1)
    before = jnp.where(r < c, 1.0, 0.0).astype(BF16)
    carry = jnp.zeros((n_exp, 1), F32)
    for s in range(ntok // SUB):
        sc = sel[:, s * SUB:(s + 1) * SUB]
        scf = jnp.where(sc, 1.0, 0.0)
        pos = (_dot(scf.astype(BF16), before) + carry).astype(I32)
        slot_ref[:, s * SUB:(s + 1) * SUB] = jnp.where(sc, pos, -1)
        cstart_ref[:, s:s + 1] = carry.astype(I32)
        carry = carry + jnp.sum(scf, axis=1, keepdims=True)


def _topk(afft, cap):
    n_exp, ntok = afft.shape
    nsub = ntok // SUB
    return pl.pallas_call(
        functools.partial(_topk_kernel, cap=cap),
        out_shape=[
            jax.ShapeDtypeStruct((n_exp, ntok), I32),
            jax.ShapeDtypeStruct((n_exp, ntok), F32),
            jax.ShapeDtypeStruct((n_exp, nsub), I32),
        ],
        compiler_params=pltpu.CompilerParams(vmem_limit_bytes=VMEM_LIMIT_BYTES),
        name="topk",
    )(afft)


def _gather_kernel(cstart_ref, slot_ref, gsel_ref, h_ref, xe_ref, gs_ref, acc_ref, gacc_ref, *, cap):
    e = pl.program_id(0)
    i = pl.program_id(1)
    n_sub = h_ref.shape[0] // SUB

    @pl.when(i == 0)
    def _():
        acc_ref[...] = jnp.zeros_like(acc_ref)
        gacc_ref[...] = jnp.zeros_like(gacc_ref)

    row = lax.broadcasted_iota(I32, (GATHER_WIN, SUB), 0)
    for s in range(n_sub):
        cs = cstart_ref[e, i * n_sub + s]
        base = pl.multiple_of(lax.div(cs, 16) * 16, 16)
        sl = slot_ref[0, :, s * SUB:(s + 1) * SUB]
        hit = row == (sl - base)
        onehot = jnp.where(hit, 1.0, 0.0).astype(BF16)
        acc_ref[pl.ds(base, GATHER_WIN), :] += _dot(onehot, h_ref[s * SUB:(s + 1) * SUB, :])
        g = gsel_ref[0, :, s * SUB:(s + 1) * SUB]
        gw = jnp.sum(jnp.where(hit, g, 0.0), axis=1, keepdims=True)
        gacc_ref[pl.ds(base, GATHER_WIN), :] += jnp.broadcast_to(gw, (GATHER_WIN, LANES))

    @pl.when(i == pl.num_programs(1) - 1)
    def _():
        xe_ref[0] = acc_ref[:cap].astype(BF16)
        gs_ref[0] = gacc_ref[:cap]


def _gather(cstart, slot3, gsel3, h2, cap):
    n_exp = slot3.shape[0]
    ntok, d = h2.shape
    tb = min(GATHER_TB, ntok)
    return pl.pallas_call(
        functools.partial(_gather_kernel, cap=cap),
        grid_spec=pltpu.PrefetchScalarGridSpec(
            num_scalar_prefetch=1,
            grid=(n_exp, ntok // tb),
            in_specs=[
                pl.BlockSpec((1, 1, tb), lambda e, i, cs: (e, 0, i)),
                pl.BlockSpec((1, 1, tb), lambda e, i, cs: (e, 0, i)),
                pl.BlockSpec((tb, d), lambda e, i, cs: (i, 0)),
            ],
            out_specs=[
                pl.BlockSpec((1, cap, d), lambda e, i, cs: (e, 0, 0)),
                pl.BlockSpec((1, cap, LANES), lambda e, i, cs: (e, 0, 0)),
            ],
            scratch_shapes=[
                pltpu.VMEM((cap + GATHER_WIN, d), F32),
                pltpu.VMEM((cap + GATHER_WIN, LANES), F32),
            ],
        ),
        out_shape=[
            jax.ShapeDtypeStruct((n_exp, cap, d), BF16),
            jax.ShapeDtypeStruct((n_exp, cap, LANES), F32),
        ],
        compiler_params=_cparams(("arbitrary", "arbitrary")),
        name="gather",
    )(cstart, slot3, gsel3, h2)


def _ffn_kernel(xe_ref, gs_ref, wg_ref, wu_ref, wd_ref, ye_ref, acc_ref):
    f = pl.program_id(1)
    x = xe_ref[0]
    g = _dot(x, wg_ref[0].astype(BF16))
    u = _dot(x, wu_ref[0].astype(BF16))
    hid = (g * jax.nn.sigmoid(g) * u).astype(BF16)
    contrib = _dot(hid, wd_ref[0].astype(BF16))

    @pl.when(f == 0)
    def _():
        acc_ref[...] = contrib

    @pl.when(f > 0)
    def _():
        acc_ref[...] += contrib

    @pl.when(f == pl.num_programs(1) - 1)
    def _():
        ye_ref[0] = (acc_ref[...] * gs_ref[0][:, :1]).astype(BF16)


def _ffn(xe, gs, w_gate, w_up, w_down):
    n_exp, cap, d = xe.shape
    ff = w_gate.shape[-1]
    tf = min(FFN_TF, ff)
    return pl.pallas_call(
        _ffn_kernel,
        grid=(n_exp, ff // tf),
        in_specs=[
            pl.BlockSpec((1, cap, d), lambda e, f: (e, 0, 0)),
            pl.BlockSpec((1, cap, LANES), lambda e, f: (e, 0, 0)),
            pl.BlockSpec((1, d, tf), lambda e, f: (e, 0, f)),
            pl.BlockSpec((1, d, tf), lambda e, f: (e, 0, f)),
            pl.BlockSpec((1, tf, d), lambda e, f: (e, f, 0)),
        ],
        out_specs=pl.BlockSpec((1, cap, d), lambda e, f: (e, 0, 0)),
        out_shape=jax.ShapeDtypeStruct((n_exp, cap, d), BF16),
        scratch_shapes=[pltpu.VMEM((cap, d), F32)],
        compiler_params=_cparams(("arbitrary", "arbitrary")),
        name="ffn",
    )(xe, gs, w_gate, w_up, w_down)


def _combine_kernel(cstart_ref, x1_ref, slot_ref, ya_ref, yb_ref, gfin_ref, y_ref, acc_ref, *, rows, nblk):
    i = pl.program_id(0)
    e = pl.program_id(1)
    n_sub = x1_ref.shape[0] // SUB
    blk_a = jnp.minimum(lax.div(cstart_ref[e, i * n_sub], rows), nblk - 1)

    @pl.when(e == 0)
    def _():
        acc_ref[...] = x1_ref[...]

    row = lax.broadcasted_iota(I32, (COMBINE_WIN, SUB), 0)
    dims = (((0,), (0,)), ((), ()))
    for s in range(n_sub):
        cs = cstart_ref[e, i * n_sub + s]
        r0 = lax.div(cs, 16) * 16 - blk_a * rows
        sl = slot_ref[0, :, s * SUB:(s + 1) * SUB]
        upd = jnp.zeros((SUB, acc_ref.shape[1]), F32)
        for which, y_ref_blk in ((0, ya_ref), (1, yb_ref)):
            start = pl.multiple_of(jnp.clip(r0 - which * rows, 0, rows - COMBINE_WIN), 16)
            first_slot = (blk_a + which) * rows + start
            hit = row == (sl - first_slot)
            onehot = jnp.where(hit, 1.0, 0.0).astype(BF16)
            win = y_ref_blk[0, pl.ds(start, COMBINE_WIN), :]
            upd = upd + lax.dot_general(onehot, win, dims, preferred_element_type=F32)
        acc_ref[s * SUB:(s + 1) * SUB, :] += upd

    @pl.when(e == pl.num_programs(1) - 1)
    def _():
        out = acc_ref[...]
        y_ref[...] = out * lax.rsqrt(jnp.mean(out * out, axis=-1, keepdims=True) + NORM_EPS) * gfin_ref[...]


def _combine(cstart, x1, slot3, ye, gfin):
    n_exp, cap, d = ye.shape
    ntok = x1.shape[0]
    tb = min(COMBINE_TB, ntok, cap)
    rows = tb
    n_sub = tb // SUB
    nblk = cap // rows

    def blk(e, i, cs, which):
        a = jnp.minimum(cs[e, i * n_sub] // rows, nblk - 1)
        return jnp.minimum(a + which, nblk - 1)

    return pl.pallas_call(
        functools.partial(_combine_kernel, rows=rows, nblk=nblk),
        grid_spec=pltpu.PrefetchScalarGridSpec(
            num_scalar_prefetch=1,
            grid=(ntok // tb, n_exp),
            in_specs=[
                pl.BlockSpec((tb, d), lambda i, e, cs: (i, 0)),
                pl.BlockSpec((1, 1, tb), lambda i, e, cs: (e, 0, i)),
                pl.BlockSpec((1, rows, d), lambda i, e, cs: (e, blk(e, i, cs, 0), 0)),
                pl.BlockSpec((1, rows, d), lambda i, e, cs: (e, blk(e, i, cs, 1), 0)),
                pl.BlockSpec((1, d), lambda i, e, cs: (0, 0)),
            ],
            out_specs=pl.BlockSpec((tb, d), lambda i, e, cs: (i, 0)),
            scratch_shapes=[pltpu.VMEM((tb, d), F32)],
        ),
        out_shape=jax.ShapeDtypeStruct((ntok, d), F32),
        compiler_params=_cparams(("arbitrary", "arbitrary")),
        name="combine",
    )(cstart, x1, slot3, ye, ye, gfin)


def _rope_tables(n):
    rows = n // GRID_W
    row = jnp.broadcast_to(jnp.arange(rows)[:, None], (rows, GRID_W)).reshape(-1).astype(F32)
    col = jnp.broadcast_to(jnp.arange(GRID_W)[None, :], (rows, GRID_W)).reshape(-1).astype(F32)
    half = HEAD_DIM // 2
    inv = ROPE_THETA ** (-jnp.arange(0, half, 2, dtype=F32) / half)
    ang_r = row[:, None] * inv[None, :]
    ang_c = col[:, None] * inv[None, :]
    cos = jnp.concatenate([jnp.cos(ang_r), jnp.cos(ang_r), jnp.cos(ang_c), jnp.cos(ang_c)], axis=-1)
    sin = jnp.concatenate([-jnp.sin(ang_r), jnp.sin(ang_r), -jnp.sin(ang_c), jnp.sin(ang_c)], axis=-1)
    return jnp.tile(cos, (1, 2)), jnp.tile(sin, (1, 2))


def _encoder(x, p):
    bsz, n, d = x.shape
    ntok = bsz * n
    n_exp = p["n_exp"]
    cap = CAPACITY_FACTOR * ntok // n_exp
    xt = x.reshape(ntok, d)
    cos, sin = _rope_tables(n)
    qa, kat, va, qb, kbt, vb = _proj(xt, p["gmix"], p["w_in"], p["gq"], p["gk"], cos, sin, p["seg"], n)
    a_out = _attn_a(qa, kat, va, bsz, n)
    b_out = _attn_b(qb, kbt, vb, p["rel_bias"], p["lam"], p["gsub"], bsz, n)
    x1, h2, afft = _outproj(xt, a_out, b_out, p["w_out"], p["gffn"], p["wr_hi"], p["wr_lo"], n_exp)
    slot, gsel, cstart = _topk(afft, cap)
    slot3 = slot.reshape(n_exp, 1, ntok)
    gsel3 = gsel.reshape(n_exp, 1, ntok)
    xe, gs = _gather(cstart, slot3, gsel3, h2, cap)
    ye = _ffn(xe, gs, p["w_gate"], p["w_up"], p["w_down"])
    y = _combine(cstart, x1, slot3, ye, p["gfin"])
    return y.reshape(bsz, n, d)


def kernel(x_prompt, x_sample, norm_mix, w_in, a_q_norm, a_k_norm, b_lambda, b_subln, w_out, rel_bias,
           norm_ffn, w_router, w_gate, w_up, w_down, norm_final):
    d = x_prompt.shape[-1]
    n_exp = w_router.shape[-1]
    wr = jnp.zeros((d, LANES), F32).at[:, :n_exp].set(w_router[0].astype(F32))
    wr_hi = wr.astype(BF16)
    lane_head = np.arange(LANES) // HEAD_DIM
    p = dict(
        n_exp=n_exp,
        gmix=norm_mix[0].astype(F32).reshape(1, d),
        w_in=w_in[0].astype(BF16),
        gq=jnp.tile(a_q_norm[0].astype(F32), 2).reshape(1, LANES),
        gk=jnp.tile(a_k_norm[0].astype(F32), 2).reshape(1, LANES),
        seg=jnp.asarray(lane_head[:, None] == lane_head[None, :], BF16),
        rel_bias=rel_bias,
        lam=b_lambda[0].astype(F32),
        gsub=b_subln[0].astype(F32).reshape(1, LANES),
        w_out=w_out[0].astype(BF16),
        gffn=norm_ffn[0].astype(F32).reshape(1, d),
        wr_hi=wr_hi,
        wr_lo=(wr - wr_hi.astype(F32)).astype(BF16),
        w_gate=w_gate[0], w_up=w_up[0], w_down=w_down[0],
        gfin=norm_final.astype(F32).reshape(1, d),
    )
    return (_encoder(x_prompt, p), _encoder(x_sample, p))
```

```python
import functools
import math

import jax
import jax.numpy as jnp
import numpy as np
from jax import lax
from jax.experimental import pallas as pl
from jax.experimental.pallas import tpu as pltpu

F32 = jnp.float32
BF16 = jnp.bfloat16
I32 = jnp.int32

HEAD_DIM = 64
A_Q_HEADS = 8
A_KV_HEADS = 2
B_HEADS = 4
A_Q_COLS = A_Q_HEADS * HEAD_DIM
A_KV_COLS = A_KV_HEADS * HEAD_DIM
B_QK_COLS = B_HEADS * 2 * HEAD_DIM
B_V_DIM = 2 * HEAD_DIM
B_V_COLS = B_HEADS * B_V_DIM
GRID_W = 64
ROPE_THETA = 10000.0
REL_BUCKETS = 32
REL_MAX_DIST = 128
CAPACITY_FACTOR = 2
NORM_EPS = 1e-6
LAMBDA_INIT = 0.8 - 0.6 * math.exp(-0.3 * 0)
QK_SCALE = HEAD_DIM ** -0.5

LANES = 128
VMEM_LIMIT_BYTES = 56 * 1024 * 1024

TOKEN_TILE = 512
ATTN_TQ = 256
ATTN_TK = 512
SUB = 128
GATHER_WIN = SUB + 16
GATHER_TB = 2048
COMBINE_TB = 512
COMBINE_WIN = 256
FFN_TF = 256
NEG_BIG = -1e30

SAFE_RANGE = 40.0
NORM_SLACK = 1.0 + 2.0 ** -7


def _dot(a, b):
    return jnp.dot(a, b, preferred_element_type=F32)


def _cparams(sem):
    return pltpu.CompilerParams(dimension_semantics=sem, vmem_limit_bytes=VMEM_LIMIT_BYTES)


def _split_dot(v, w_bf16):
    hi = v.astype(BF16)
    lo = (v - hi.astype(F32)).astype(BF16)
    return _dot(hi, w_bf16) + _dot(lo, w_bf16)


def _proj_kernel(x_ref, gmix_ref, w_ref, gq_ref, gk_ref, cos_ref, sin_ref, seg_ref,
                 qa_ref, kat_ref, va_ref, qb_ref, kbt_ref, vb_ref, st_ref):
    tm = x_ref.shape[0]
    x = x_ref[...]
    h = (x * lax.rsqrt(jnp.mean(x * x, axis=-1, keepdims=True) + NORM_EPS) * gmix_ref[...]).astype(BF16)
    cos = cos_ref[...]
    sin = sin_ref[...]
    seg = seg_ref[...]
    lane = lax.broadcasted_iota(I32, (1, LANES), 1)
    first_half = (lane % 32) < 16
    low_head = lane < HEAD_DIM
    ones_rows = jnp.where(lax.broadcasted_iota(I32, (HEAD_DIM, tm), 0) == 0, 1.0, 0.0)
    ones_lane64 = jnp.where(lane == HEAD_DIM, 1.0, 0.0)
    ones_lane0 = jnp.broadcast_to(jnp.where(lane == 0, 1.0, 0.0), (tm, LANES)).astype(BF16)

    def norm_rope(y, gain):
        ss = _split_dot(y * y, seg)
        y = y * lax.rsqrt(ss * (1.0 / HEAD_DIM) + NORM_EPS) * gain
        rot = jnp.where(first_half, pltpu.roll(y, LANES - 16, 1), pltpu.roll(y, 16, 1))
        return y * cos + rot * sin

    def max_head_sumsq(vals):
        sq = vals.astype(F32)
        return jnp.max(_dot((sq * sq).astype(BF16), seg), axis=0, keepdims=True)

    def with_ones_row(kt_half):
        return jnp.concatenate([kt_half, ones_rows], axis=0).astype(BF16)

    c0 = 0
    aq = _dot(h, w_ref[:, c0:c0 + A_Q_COLS])
    qa_stat = jnp.zeros((1, LANES), F32)
    for c in range(A_Q_COLS // LANES):
        y = (norm_rope(aq[:, c * LANES:(c + 1) * LANES], gq_ref[...]) * QK_SCALE).astype(BF16)
        qa_ref[:, c * LANES:(c + 1) * LANES] = y
        qa_stat = jnp.maximum(qa_stat, max_head_sumsq(y))
    c0 += A_Q_COLS

    akv = _dot(h, w_ref[:, c0:c0 + 2 * A_KV_COLS])
    k = norm_rope(akv[:, :LANES], gk_ref[...]).astype(BF16)
    ka_stat = max_head_sumsq(k)
    kt = k.astype(F32).T
    for g in range(A_KV_HEADS):
        kat_ref[g] = with_ones_row(kt[g * HEAD_DIM:(g + 1) * HEAD_DIM])
    v = akv[:, LANES:]
    va_ref[0] = jnp.where(low_head, v, ones_lane64).astype(BF16)
    va_ref[1] = jnp.where(low_head, pltpu.roll(v, HEAD_DIM, 1), ones_lane64).astype(BF16)
    c0 += 2 * A_KV_COLS

    qb = (_dot(h, w_ref[:, c0:c0 + B_QK_COLS]) * QK_SCALE).astype(BF16)
    qb_ref[...] = qb
    c0 += B_QK_COLS
    bk = _dot(h, w_ref[:, c0:c0 + B_QK_COLS]).astype(BF16)
    qb_stat = jnp.zeros((1, LANES), F32)
    kb_stat = jnp.zeros((1, LANES), F32)
    for hd in range(B_HEADS):
        blk = slice(hd * LANES, (hd + 1) * LANES)
        qb_stat = jnp.maximum(qb_stat, max_head_sumsq(qb[:, blk]))
        kb_stat = jnp.maximum(kb_stat, max_head_sumsq(bk[:, blk]))
        kt = bk[:, blk].astype(F32).T
        kbt_ref[hd, 0] = with_ones_row(kt[:HEAD_DIM])
        kbt_ref[hd, 1] = with_ones_row(kt[HEAD_DIM:])
    c0 += B_QK_COLS
    bv = _dot(h, w_ref[:, c0:c0 + B_V_COLS]).astype(BF16)
    for hd in range(B_HEADS):
        vb_ref[:, 2 * hd * LANES:(2 * hd + 1) * LANES] = bv[:, hd * LANES:(hd + 1) * LANES]
        vb_ref[:, (2 * hd + 1) * LANES:(2 * hd + 2) * LANES] = ones_lane0

    stats = [jnp.broadcast_to(jnp.max(s, axis=1, keepdims=True), (1, LANES))
             for s in (qa_stat, ka_stat, qb_stat, kb_stat)]
    st_ref[0] = jnp.concatenate(stats + [jnp.zeros((8 - len(stats), LANES), F32)], axis=0)


def _proj(x, gmix, w_in, gq, gk, cos, sin, seg, n):
    ntok, d = x.shape
    tm = min(TOKEN_TILE, n)
    per_seq = n // tm
    ntiles = ntok // tm
    tok = lambda i: (i, 0)
    full = lambda i: (0, 0)
    return pl.pallas_call(
        _proj_kernel,
        grid=(ntiles,),
        in_specs=[
            pl.BlockSpec((tm, d), tok),
            pl.BlockSpec((1, d), full),
            pl.BlockSpec(w_in.shape, full),
            pl.BlockSpec((1, LANES), full),
            pl.BlockSpec((1, LANES), full),
            pl.BlockSpec((tm, LANES), lambda i: (i % per_seq, 0)),
            pl.BlockSpec((tm, LANES), lambda i: (i % per_seq, 0)),
            pl.BlockSpec((LANES, LANES), full),
        ],
        out_specs=[
            pl.BlockSpec((tm, A_Q_COLS), tok),
            pl.BlockSpec((A_KV_HEADS, LANES, tm), lambda i: (0, 0, i)),
            pl.BlockSpec((A_KV_HEADS, tm, LANES), lambda i: (0, i, 0)),
            pl.BlockSpec((tm, B_QK_COLS), tok),
            pl.BlockSpec((B_HEADS, 2, LANES, tm), lambda i: (0, 0, 0, i)),
            pl.BlockSpec((tm, 2 * B_V_COLS), tok),
            pl.BlockSpec((1, 8, LANES), lambda i: (i, 0, 0)),
        ],
        out_shape=[
            jax.ShapeDtypeStruct((ntok, A_Q_COLS), BF16),
            jax.ShapeDtypeStruct((A_KV_HEADS, LANES, ntok), BF16),
            jax.ShapeDtypeStruct((A_KV_HEADS, ntok, LANES), BF16),
            jax.ShapeDtypeStruct((ntok, B_QK_COLS), BF16),
            jax.ShapeDtypeStruct((B_HEADS, 2, LANES, ntok), BF16),
            jax.ShapeDtypeStruct((ntok, 2 * B_V_COLS), BF16),
            jax.ShapeDtypeStruct((ntiles, 8, LANES), F32),
        ],
        compiler_params=_cparams(("arbitrary",)),
        name="proj",
    )(x, gmix, w_in, gq, gk, cos, sin, seg)


def _head_operands(q_pair, kmax, fixed_shift):
    lane = lax.broadcasted_iota(I32, (1, LANES), 1)
    qp = q_pair.astype(F32)
    out = []
    for qh in (qp, pltpu.roll(qp, HEAD_DIM, 1)):
        qh = jnp.where(lane < HEAD_DIM, qh, 0.0)
        if fixed_shift:
            shift = jnp.sqrt(jnp.sum(qh * qh, axis=-1, keepdims=True)) * kmax
            qh = jnp.where(lane == HEAD_DIM, -shift, qh)
        out.append(qh.astype(BF16))
    return out


def _online_step(q, kt, v, state, bias=None):
    m, acc = state
    s = _dot(q, kt)
    if bias is not None:
        s = s + bias
    m_new = jnp.maximum(m, jnp.max(s, axis=-1, keepdims=True))
    acc = jnp.exp(m - m_new) * acc + _dot(jnp.exp(s - m_new).astype(BF16), v)
    return m_new, acc


def _two_slot_pipeline(nk, s_ref, score_fn, weigh_fn, accs):
    n_maps = s_ref.shape[1]

    def put(slot, j):
        for h, s in enumerate(score_fn(j)):
            s_ref[slot, h] = s

    def take(slot, j, accs):
        return weigh_fn(j, tuple(s_ref[slot, h] for h in range(n_maps)), accs)

    def body(i, accs):
        put(1, 2 * i + 1)
        accs = take(0, 2 * i, accs)
        put(0, 2 * i + 2)
        return take(1, 2 * i + 1, accs)

    put(0, 0)
    accs = lax.fori_loop(0, nk // 2 - 1, body, accs)
    put(1, nk - 1)
    accs = take(0, nk - 2, accs)
    return take(1, nk - 1, accs)


def _attn_a_kernel(kmax_ref, q_ref, kt_ref, v_ref, o_ref, *scratch, tk, fixed_shift):
    tq = q_ref.shape[0]
    nk = kt_ref.shape[-1] // tk
    n_pairs = q_ref.shape[1] // LANES
    kmax = kmax_ref[:, :1]
    qs = []
    for pr in range(n_pairs):
        qs += _head_operands(q_ref[:, pr * LANES:(pr + 1) * LANES], kmax, fixed_shift)

    def tiles(j):
        off = pl.multiple_of(j * tk, tk)
        return kt_ref[0, :, pl.ds(off, tk)], v_ref[0, pl.ds(off, tk), :]

    zero = jnp.zeros((tq, LANES), F32)
    if fixed_shift:
        def scores(j):
            return tuple(_dot(q, tiles(j)[0]) for q in qs)

        def weighted(j, ss, accs):
            v = tiles(j)[1]
            return tuple(acc + _dot(jnp.exp(s).astype(BF16), v) for s, acc in zip(ss, accs))

        accs = _two_slot_pipeline(nk, scratch[0], scores, weighted, tuple(zero for _ in qs))
    else:
        def body(j, states):
            kt, v = tiles(j)
            return tuple(_online_step(q, kt, v, st) for q, st in zip(qs, states))
        init = (jnp.full((tq, 1), NEG_BIG, F32), zero)
        accs = [st[1] for st in lax.fori_loop(0, nk, body, tuple(init for _ in qs))]

    low_head = lax.broadcasted_iota(I32, (1, LANES), 1) < HEAD_DIM
    for pr in range(n_pairs):
        even = accs[2 * pr] / accs[2 * pr][:, HEAD_DIM:HEAD_DIM + 1]
        odd = accs[2 * pr + 1] / accs[2 * pr + 1][:, HEAD_DIM:HEAD_DIM + 1]
        o_ref[:, pr * LANES:(pr + 1) * LANES] = jnp.where(
            low_head, even, pltpu.roll(odd, HEAD_DIM, 1)).astype(BF16)


def _score_scratch(fixed_shift, n_maps, tq, tk, n):
    if not fixed_shift:
        return []
    assert (n // tk) % 2 == 0, "the two-slot score pipeline walks key tiles in pairs"
    return [pltpu.VMEM((2, n_maps, tq, tk), F32)]


def _attn_a(kmax, qa, kat, va, *, bsz, n, fixed_shift):
    ntok = qa.shape[0]
    tq = min(ATTN_TQ, n)
    tk = min(ATTN_TK, n)
    nq = n // tq
    gcols = A_Q_COLS // A_KV_HEADS
    return pl.pallas_call(
        functools.partial(_attn_a_kernel, tk=tk, fixed_shift=fixed_shift),
        grid=(bsz, A_KV_HEADS, nq),
        in_specs=[
            pl.BlockSpec((1, LANES), lambda b, g, i: (0, 0)),
            pl.BlockSpec((tq, gcols), lambda b, g, i: (b * nq + i, g)),
            pl.BlockSpec((1, LANES, n), lambda b, g, i: (g, 0, b)),
            pl.BlockSpec((1, n, LANES), lambda b, g, i: (g, b, 0)),
        ],
        out_specs=pl.BlockSpec((tq, gcols), lambda b, g, i: (b * nq + i, g)),
        out_shape=jax.ShapeDtypeStruct((ntok, A_Q_COLS), BF16),
        scratch_shapes=_score_scratch(fixed_shift, A_Q_HEADS // A_KV_HEADS, tq, tk, n),
        compiler_params=_cparams(("arbitrary", "arbitrary", "arbitrary")),
        name="attn_a_shift" if fixed_shift else "attn_a_online",
    )(kmax, qa, kat, va)


def _attn_b_kernel(kmax_ref, q_ref, kt_ref, v_ref, bias_ref, lam_ref, gsub_ref, o_ref, *scratch,
                   tk, d0, dstep, fixed_shift):
    tq = q_ref.shape[0]
    nk = kt_ref.shape[-1] // tk
    n_tiles = bias_ref.shape[1]
    qi = pl.program_id(2)
    qs = _head_operands(q_ref[...], kmax_ref[:, :1], fixed_shift)

    def tiles(j):
        off = pl.multiple_of(j * tk, tk)
        return (kt_ref[0, 0, :, pl.ds(off, tk)], kt_ref[0, 1, :, pl.ds(off, tk)]), v_ref[pl.ds(off, tk), :]

    def bias_tile(j):
        u = jnp.clip(j * tk - qi * tq - d0 + dstep, 0, (n_tiles - 1) * dstep)
        return bias_ref[0, lax.div(u, dstep)]

    zero = jnp.zeros((tq, 2 * LANES), F32)
    if fixed_shift:
        def scores(j):
            return tuple(_dot(q, kt) for q, kt in zip(qs, tiles(j)[0]))

        def weighted(j, ss, accs):
            v = tiles(j)[1]
            bias = bias_tile(j)
            return tuple(acc + _dot(jnp.exp(s + bias).astype(BF16), v) for s, acc in zip(ss, accs))

        accs = _two_slot_pipeline(nk, scratch[0], scores, weighted, (zero, zero))
    else:
        def body(j, states):
            kts, v = tiles(j)
            bias = bias_tile(j)
            return tuple(_online_step(q, kt, v, st, bias) for q, kt, st in zip(qs, kts, states))
        init = (jnp.full((tq, 1), NEG_BIG, F32), zero)
        accs = [st[1] for st in lax.fori_loop(0, nk, body, (init, init))]

    o1 = accs[0][:, :B_V_DIM] / accs[0][:, B_V_DIM:B_V_DIM + 1]
    o2 = accs[1][:, :B_V_DIM] / accs[1][:, B_V_DIM:B_V_DIM + 1]
    lp = lam_ref[...]
    lam = (jnp.exp(jnp.sum(lp[0:1] * lp[1:2], axis=-1, keepdims=True))
           - jnp.exp(jnp.sum(lp[2:3] * lp[3:4], axis=-1, keepdims=True)) + LAMBDA_INIT)
    out = o1 - lam * o2
    out = out * lax.rsqrt(jnp.mean(out * out, axis=-1, keepdims=True) + NORM_EPS) * gsub_ref[...]
    o_ref[...] = (out * (1.0 - LAMBDA_INIT)).astype(BF16)


def _t5_bucket(rel):
    nb = REL_BUCKETS // 2
    ret = (rel > 0).astype(I32) * nb
    n = jnp.abs(rel)
    max_exact = nb // 2
    nf = jnp.maximum(n, 1).astype(F32)
    large = max_exact + (jnp.log(nf / max_exact) / math.log(REL_MAX_DIST / max_exact)
                         * (nb - max_exact)).astype(I32)
    large = jnp.minimum(large, nb - 1)
    return ret + jnp.where(n < max_exact, n, large)


def _bias_tiles(rel_bias, tq, tk):
    dstep = math.gcd(tq, tk)
    reach = (tq + tk + REL_MAX_DIST) // dstep
    band = [d for d in range(-reach * dstep, reach * dstep + 1, dstep)
            if d + tk - 1 > -REL_MAX_DIST and d - (tq - 1) < REL_MAX_DIST]
    far = tq + tk + REL_MAX_DIST
    ds = jnp.asarray([-far] + band + [far], I32)
    rel = (ds[:, None, None] + jnp.arange(tk, dtype=I32)[None, None, :]
           - jnp.arange(tq, dtype=I32)[None, :, None])
    bucket = _t5_bucket(rel)
    table = rel_bias.astype(F32)
    table = table - jnp.max(table, axis=0, keepdims=True)
    tiles = jnp.zeros((table.shape[1],) + bucket.shape, F32)
    for b in range(REL_BUCKETS):
        tiles = jnp.where(bucket[None] == b, table[b][:, None, None, None], tiles)
    return tiles, band[0], dstep


def _attn_b(kmax, qb, kbt, vb, tiles, lam_params, gsub, *, bsz, n, d0, dstep, fixed_shift):
    ntok = qb.shape[0]
    tq, tk = tiles.shape[2], tiles.shape[3]
    nq = n // tq
    return pl.pallas_call(
        functools.partial(_attn_b_kernel, tk=tk, d0=d0, dstep=dstep, fixed_shift=fixed_shift),
        grid=(bsz, B_HEADS, nq),
        in_specs=[
            pl.BlockSpec((1, LANES), lambda b, h, i: (0, 0)),
            pl.BlockSpec((tq, LANES), lambda b, h, i: (b * nq + i, h)),
            pl.BlockSpec((1, 2, LANES, n), lambda b, h, i: (h, 0, 0, b)),
            pl.BlockSpec((n, 2 * LANES), lambda b, h, i: (b, h)),
            pl.BlockSpec((1,) + tiles.shape[1:], lambda b, h, i: (h, 0, 0, 0)),
            pl.BlockSpec(lam_params.shape, lambda b, h, i: (0, 0)),
            pl.BlockSpec((1, LANES), lambda b, h, i: (0, 0)),
        ],
        out_specs=pl.BlockSpec((tq, LANES), lambda b, h, i: (b * nq + i, h)),
        out_shape=jax.ShapeDtypeStruct((ntok, B_V_COLS), BF16),
        scratch_shapes=_score_scratch(fixed_shift, 2, tq, tk, n),
        compiler_params=_cparams(("arbitrary", "arbitrary", "arbitrary")),
        name="attn_b_shift" if fixed_shift else "attn_b_online",
    )(kmax, qb, kbt, vb, tiles, lam_params, gsub)


def _outproj_kernel(x_ref, a_ref, b_ref, wo_ref, gffn_ref, wr_hi_ref, wr_lo_ref,
                    x1_ref, h2_ref, afft_ref, *, n_experts):
    wa = wo_ref.shape[0] // 2
    x1 = x_ref[...] + _dot(a_ref[...], wo_ref[:wa]) + _dot(b_ref[...], wo_ref[wa:])
    x1_ref[...] = x1
    h2 = x1 * lax.rsqrt(jnp.mean(x1 * x1, axis=-1, keepdims=True) + NORM_EPS) * gffn_ref[...]
    h2_ref[...] = h2.astype(BF16)
    hi = h2.astype(BF16)
    lo = (h2 - hi.astype(F32)).astype(BF16)
    logits = _dot(hi, wr_hi_ref[...]) + _dot(hi, wr_lo_ref[...]) + _dot(lo, wr_hi_ref[...])
    lane = lax.broadcasted_iota(I32, (1, LANES), 1)
    logits = jnp.where(lane < n_experts, logits, NEG_BIG)
    ex = jnp.exp(logits - jnp.max(logits, axis=-1, keepdims=True))
    aff = ex / jnp.sum(ex, axis=-1, keepdims=True)
    afft_ref[...] = aff.T[:n_experts]


def _outproj(x, a_out, b_out, w_out, gffn, wr_hi, wr_lo, n_experts):
    ntok, d = x.shape
    tm = min(TOKEN_TILE, ntok)
    tok = lambda i: (i, 0)
    full = lambda i: (0, 0)
    return pl.pallas_call(
        functools.partial(_outproj_kernel, n_experts=n_experts),
        grid=(ntok // tm,),
        in_specs=[
            pl.BlockSpec((tm, d), tok),
            pl.BlockSpec((tm, a_out.shape[1]), tok),
            pl.BlockSpec((tm, b_out.shape[1]), tok),
            pl.BlockSpec(w_out.shape, full),
            pl.BlockSpec((1, d), full),
            pl.BlockSpec(wr_hi.shape, full),
            pl.BlockSpec(wr_lo.shape, full),
        ],
        out_specs=[
            pl.BlockSpec((tm, d), tok),
            pl.BlockSpec((tm, d), tok),
            pl.BlockSpec((n_experts, tm), lambda i: (0, i)),
        ],
        out_shape=[
            jax.ShapeDtypeStruct((ntok, d), F32),
            jax.ShapeDtypeStruct((ntok, d), BF16),
            jax.ShapeDtypeStruct((n_experts, ntok), F32),
        ],
        compiler_params=_cparams(("arbitrary",)),
        name="outproj",
    )(x, a_out, b_out, w_out, gffn, wr_hi, wr_lo)


def _topk_kernel(aff_ref, slot_ref, gsel_ref, cstart_ref, *, cap):
    aff = aff_ref[...]
    n_exp, ntok = aff.shape
    bits = lax.bitcast_convert_type(aff, I32)

    def count(mask):
        return jnp.sum(mask.astype(I32), axis=1, keepdims=True)

    def value_bit(i, t):
        cand = t | lax.shift_left(jnp.int32(1), 30 - i)
        return jnp.where(count(bits >= cand) >= cap, cand, t)

    thr = lax.fori_loop(0, 31, value_bit, jnp.zeros((n_exp, 1), I32))
    above = bits > thr
    tie = bits == thr
    need = cap - count(above)
    idx = lax.broadcasted_iota(I32, aff.shape, 1)

    n_bits = int(ntok).bit_length()

    def index_bit(i, bound):
        cand = bound + lax.shift_left(jnp.int32(1), n_bits - 1 - i)
        ok = (cand <= ntok) & (count(tie & (idx < cand)) <= need)
        return jnp.where(ok, cand, bound)

    bound = lax.fori_loop(0, n_bits, index_bit, jnp.zeros((n_exp, 1), I32))
    sel = above | (tie & (idx < bound))
    gsel_ref[...] = jnp.where(sel, aff, 0.0)

    r = lax.broadcasted_iota(I32, (SUB, SUB), 0)
    c = lax.broadcasted_iota(I32, (SUB, SUB), 1)
    before = jnp.where(r < c, 1.0, 0.0).astype(BF16)
    carry = jnp.zeros((n_exp, 1), F32)
    for s in range(ntok // SUB):
        sc = sel[:, s * SUB:(s + 1) * SUB]
        scf = jnp.where(sc, 1.0, 0.0)
        pos = (_dot(scf.astype(BF16), before) + carry).astype(I32)
        slot_ref[:, s * SUB:(s + 1) * SUB] = jnp.where(sc, pos, -1)
        cstart_ref[:, s:s + 1] = carry.astype(I32)
        carry = carry + jnp.sum(scf, axis=1, keepdims=True)


def _topk(afft, cap):
    n_exp, ntok = afft.shape
    nsub = ntok // SUB
    return pl.pallas_call(
        functools.partial(_topk_kernel, cap=cap),
        out_shape=[
            jax.ShapeDtypeStruct((n_exp, ntok), I32),
            jax.ShapeDtypeStruct((n_exp, ntok), F32),
            jax.ShapeDtypeStruct((n_exp, nsub), I32),
        ],
        compiler_params=pltpu.CompilerParams(vmem_limit_bytes=VMEM_LIMIT_BYTES),
        name="topk",
    )(afft)


def _gather_kernel(cstart_ref, slot_ref, gsel_ref, h_ref, xe_ref, gs_ref, acc_ref, gacc_ref, *, cap):
    e = pl.program_id(0)
    i = pl.program_id(1)
    n_sub = h_ref.shape[0] // SUB

    @pl.when(i == 0)
    def _():
        acc_ref[...] = jnp.zeros_like(acc_ref)
        gacc_ref[...] = jnp.zeros_like(gacc_ref)

    row = lax.broadcasted_iota(I32, (GATHER_WIN, SUB), 0)
    for s in range(n_sub):
        cs = cstart_ref[e, i * n_sub + s]
        base = pl.multiple_of(lax.div(cs, 16) * 16, 16)
        sl = slot_ref[0, :, s * SUB:(s + 1) * SUB]
        hit = row == (sl - base)
        onehot = jnp.where(hit, 1.0, 0.0).astype(BF16)
        acc_ref[pl.ds(base, GATHER_WIN), :] += _dot(onehot, h_ref[s * SUB:(s + 1) * SUB, :])
        g = gsel_ref[0, :, s * SUB:(s + 1) * SUB]
        gw = jnp.sum(jnp.where(hit, g, 0.0), axis=1, keepdims=True)
        gacc_ref[pl.ds(base, GATHER_WIN), :] += jnp.broadcast_to(gw, (GATHER_WIN, LANES))

    @pl.when(i == pl.num_programs(1) - 1)
    def _():
        xe_ref[0] = acc_ref[:cap].astype(BF16)
        gs_ref[0] = gacc_ref[:cap]


def _gather(cstart, slot3, gsel3, h2, cap):
    n_exp = slot3.shape[0]
    ntok, d = h2.shape
    tb = min(GATHER_TB, ntok)
    return pl.pallas_call(
        functools.partial(_gather_kernel, cap=cap),
        grid_spec=pltpu.PrefetchScalarGridSpec(
            num_scalar_prefetch=1,
            grid=(n_exp, ntok // tb),
            in_specs=[
                pl.BlockSpec((1, 1, tb), lambda e, i, cs: (e, 0, i)),
                pl.BlockSpec((1, 1, tb), lambda e, i, cs: (e, 0, i)),
                pl.BlockSpec((tb, d), lambda e, i, cs: (i, 0)),
            ],
            out_specs=[
                pl.BlockSpec((1, cap, d), lambda e, i, cs: (e, 0, 0)),
                pl.BlockSpec((1, cap, LANES), lambda e, i, cs: (e, 0, 0)),
            ],
            scratch_shapes=[
                pltpu.VMEM((cap + GATHER_WIN, d), F32),
                pltpu.VMEM((cap + GATHER_WIN, LANES), F32),
            ],
        ),
        out_shape=[
            jax.ShapeDtypeStruct((n_exp, cap, d), BF16),
            jax.ShapeDtypeStruct((n_exp, cap, LANES), F32),
        ],
        compiler_params=_cparams(("arbitrary", "arbitrary")),
        name="gather",
    )(cstart, slot3, gsel3, h2)


def _ffn_kernel(xe_ref, gs_ref, wg_ref, wu_ref, wd_ref, ye_ref, acc_ref):
    f = pl.program_id(1)
    x = xe_ref[0]
    g = _dot(x, wg_ref[0].astype(BF16))
    u = _dot(x, wu_ref[0].astype(BF16))
    hid = (g * jax.nn.sigmoid(g) * u).astype(BF16)
    contrib = _dot(hid, wd_ref[0].astype(BF16))

    @pl.when(f == 0)
    def _():
        acc_ref[...] = contrib

    @pl.when(f > 0)
    def _():
        acc_ref[...] += contrib

    @pl.when(f == pl.num_programs(1) - 1)
    def _():
        ye_ref[0] = (acc_ref[...] * gs_ref[0][:, :1]).astype(BF16)


def _ffn(xe, gs, w_gate, w_up, w_down):
    n_exp, cap, d = xe.shape
    ff = w_gate.shape[-1]
    tf = min(FFN_TF, ff)
    return pl.pallas_call(
        _ffn_kernel,
        grid=(n_exp, ff // tf),
        in_specs=[
            pl.BlockSpec((1, cap, d), lambda e, f: (e, 0, 0)),
            pl.BlockSpec((1, cap, LANES), lambda e, f: (e, 0, 0)),
            pl.BlockSpec((1, d, tf), lambda e, f: (e, 0, f)),
            pl.BlockSpec((1, d, tf), lambda e, f: (e, 0, f)),
            pl.BlockSpec((1, tf, d), lambda e, f: (e, f, 0)),
        ],
        out_specs=pl.BlockSpec((1, cap, d), lambda e, f: (e, 0, 0)),
        out_shape=jax.ShapeDtypeStruct((n_exp, cap, d), BF16),
        scratch_shapes=[pltpu.VMEM((cap, d), F32)],
        compiler_params=_cparams(("arbitrary", "arbitrary")),
        name="ffn",
    )(xe, gs, w_gate, w_up, w_down)


def _combine_kernel(cstart_ref, x1_ref, slot_ref, ya_ref, yb_ref, gfin_ref, y_ref, acc_ref, *, rows, nblk):
    i = pl.program_id(0)
    e = pl.program_id(1)
    n_sub = x1_ref.shape[0] // SUB
    blk_a = jnp.minimum(lax.div(cstart_ref[e, i * n_sub], rows), nblk - 1)

    @pl.when(e == 0)
    def _():
        acc_ref[...] = x1_ref[...]

    row = lax.broadcasted_iota(I32, (COMBINE_WIN, SUB), 0)
    dims = (((0,), (0,)), ((), ()))
    for s in range(n_sub):
        cs = cstart_ref[e, i * n_sub + s]
        r0 = lax.div(cs, 16) * 16 - blk_a * rows
        sl = slot_ref[0, :, s * SUB:(s + 1) * SUB]
        upd = jnp.zeros((SUB, acc_ref.shape[1]), F32)
        for which, y_ref_blk in ((0, ya_ref), (1, yb_ref)):
            start = pl.multiple_of(jnp.clip(r0 - which * rows, 0, rows - COMBINE_WIN), 16)
            first_slot = (blk_a + which) * rows + start
            hit = row == (sl - first_slot)
            onehot = jnp.where(hit, 1.0, 0.0).astype(BF16)
            win = y_ref_blk[0, pl.ds(start, COMBINE_WIN), :]
            upd = upd + lax.dot_general(onehot, win, dims, preferred_element_type=F32)
        acc_ref[s * SUB:(s + 1) * SUB, :] += upd

    @pl.when(e == pl.num_programs(1) - 1)
    def _():
        out = acc_ref[...]
        y_ref[...] = out * lax.rsqrt(jnp.mean(out * out, axis=-1, keepdims=True) + NORM_EPS) * gfin_ref[...]


def _combine(cstart, x1, slot3, ye, gfin):
    n_exp, cap, d = ye.shape
    ntok = x1.shape[0]
    tb = min(COMBINE_TB, ntok, cap)
    rows = tb
    n_sub = tb // SUB
    nblk = cap // rows

    def blk(e, i, cs, which):
        a = jnp.minimum(cs[e, i * n_sub] // rows, nblk - 1)
        return jnp.minimum(a + which, nblk - 1)

    return pl.pallas_call(
        functools.partial(_combine_kernel, rows=rows, nblk=nblk),
        grid_spec=pltpu.PrefetchScalarGridSpec(
            num_scalar_prefetch=1,
            grid=(ntok // tb, n_exp),
            in_specs=[
                pl.BlockSpec((tb, d), lambda i, e, cs: (i, 0)),
                pl.BlockSpec((1, 1, tb), lambda i, e, cs: (e, 0, i)),
                pl.BlockSpec((1, rows, d), lambda i, e, cs: (e, blk(e, i, cs, 0), 0)),
                pl.BlockSpec((1, rows, d), lambda i, e, cs: (e, blk(e, i, cs, 1), 0)),
                pl.BlockSpec((1, d), lambda i, e, cs: (0, 0)),
            ],
            out_specs=pl.BlockSpec((tb, d), lambda i, e, cs: (i, 0)),
            scratch_shapes=[pltpu.VMEM((tb, d), F32)],
        ),
        out_shape=jax.ShapeDtypeStruct((ntok, d), F32),
        compiler_params=_cparams(("arbitrary", "arbitrary")),
        name="combine",
    )(cstart, x1, slot3, ye, ye, gfin)


def _rope_tables(n):
    rows = n // GRID_W
    row = jnp.broadcast_to(jnp.arange(rows)[:, None], (rows, GRID_W)).reshape(-1).astype(F32)
    col = jnp.broadcast_to(jnp.arange(GRID_W)[None, :], (rows, GRID_W)).reshape(-1).astype(F32)
    half = HEAD_DIM // 2
    inv = ROPE_THETA ** (-jnp.arange(0, half, 2, dtype=F32) / half)
    ang_r = row[:, None] * inv[None, :]
    ang_c = col[:, None] * inv[None, :]
    cos = jnp.concatenate([jnp.cos(ang_r), jnp.cos(ang_r), jnp.cos(ang_c), jnp.cos(ang_c)], axis=-1)
    sin = jnp.concatenate([-jnp.sin(ang_r), jnp.sin(ang_r), -jnp.sin(ang_c), jnp.sin(ang_c)], axis=-1)
    return jnp.tile(cos, (1, 2)), jnp.tile(sin, (1, 2))


def _encoder(x, p):
    bsz, n, d = x.shape
    ntok = bsz * n
    n_exp = p["n_exp"]
    cap = CAPACITY_FACTOR * ntok // n_exp
    xt = x.reshape(ntok, d)
    cos, sin = _rope_tables(n)
    qa, kat, va, qb, kbt, vb, stats = _proj(xt, p["gmix"], p["w_in"], p["gq"], p["gk"], cos, sin,
                                            p["seg"], n)

    qa_n, ka_n, qb_n, kb_n = jnp.sqrt(jnp.max(stats[:, :4, 0], axis=0)) * NORM_SLACK
    lanes = lambda v: jnp.broadcast_to(v, (1, LANES)).astype(F32)
    a_args = (lanes(ka_n), qa, kat, va)
    a_out = lax.cond(qa_n * ka_n <= SAFE_RANGE,
                     functools.partial(_attn_a, bsz=bsz, n=n, fixed_shift=True),
                     functools.partial(_attn_a, bsz=bsz, n=n, fixed_shift=False), *a_args)
    tiles, d0, dstep = p["bias_tiles"](min(ATTN_TQ, n), min(ATTN_TK, n))
    b_args = (lanes(kb_n), qb, kbt, vb, tiles, p["lam"], p["gsub"])
    b_kw = dict(bsz=bsz, n=n, d0=d0, dstep=dstep)
    b_out = lax.cond(2.0 * qb_n * kb_n + p["bias_spread"] <= 2.0 * SAFE_RANGE,
                     functools.partial(_attn_b, fixed_shift=True, **b_kw),
                     functools.partial(_attn_b, fixed_shift=False, **b_kw), *b_args)

    x1, h2, afft = _outproj(xt, a_out, b_out, p["w_out"], p["gffn"], p["wr_hi"], p["wr_lo"], n_exp)
    slot, gsel, cstart = _topk(afft, cap)
    slot3 = slot.reshape(n_exp, 1, ntok)
    gsel3 = gsel.reshape(n_exp, 1, ntok)
    xe, gs = _gather(cstart, slot3, gsel3, h2, cap)
    ye = _ffn(xe, gs, p["w_gate"], p["w_up"], p["w_down"])
    y = _combine(cstart, x1, slot3, ye, p["gfin"])
    return y.reshape(bsz, n, d)


def kernel(x_prompt, x_sample, norm_mix, w_in, a_q_norm, a_k_norm, b_lambda, b_subln, w_out, rel_bias,
           norm_ffn, w_router, w_gate, w_up, w_down, norm_final):
    d = x_prompt.shape[-1]
    n_exp = w_router.shape[-1]
    wr = jnp.zeros((d, LANES), F32).at[:, :n_exp].set(w_router[0].astype(F32))
    wr_hi = wr.astype(BF16)
    lane_head = np.arange(LANES) // HEAD_DIM
    tile_cache = {}

    def bias_tiles(tq, tk):
        if (tq, tk) not in tile_cache:
            tile_cache[(tq, tk)] = _bias_tiles(rel_bias, tq, tk)
        return tile_cache[(tq, tk)]

    p = dict(
        n_exp=n_exp,
        gmix=norm_mix[0].astype(F32).reshape(1, d),
        w_in=w_in[0].astype(BF16),
        gq=jnp.tile(a_q_norm[0].astype(F32), 2).reshape(1, LANES),
        gk=jnp.tile(a_k_norm[0].astype(F32), 2).reshape(1, LANES),
        seg=jnp.asarray(lane_head[:, None] == lane_head[None, :], BF16),
        bias_tiles=bias_tiles,
        bias_spread=(jnp.max(rel_bias) - jnp.min(rel_bias)).astype(F32),
        lam=b_lambda[0].astype(F32),
        gsub=b_subln[0].astype(F32).reshape(1, LANES),
        w_out=w_out[0].astype(BF16),
        gffn=norm_ffn[0].astype(F32).reshape(1, d),
        wr_hi=wr_hi,
        wr_lo=(wr - wr_hi.astype(F32)).astype(BF16),
        w_gate=w_gate[0], w_up=w_up[0], w_down=w_down[0],
        gfin=norm_final.astype(F32).reshape(1, d),
    )
    return (_encoder(x_prompt, p), _encoder(x_sample, p))
```

```python
import functools
import math

import jax
import jax.numpy as jnp
import numpy as np
from jax import lax
from jax.experimental import pallas as pl
from jax.experimental.pallas import tpu as pltpu

F32 = jnp.float32
BF16 = jnp.bfloat16
I32 = jnp.int32

HEAD_DIM = 64
A_Q_HEADS = 8
A_KV_HEADS = 2
B_HEADS = 4
A_Q_COLS = A_Q_HEADS * HEAD_DIM
A_KV_COLS = A_KV_HEADS * HEAD_DIM
B_QK_COLS = B_HEADS * 2 * HEAD_DIM
B_V_DIM = 2 * HEAD_DIM
B_V_COLS = B_HEADS * B_V_DIM
GRID_W = 64
ROPE_THETA = 10000.0
REL_BUCKETS = 32
REL_MAX_DIST = 128
CAPACITY_FACTOR = 2
NORM_EPS = 1e-6
LAMBDA_INIT = 0.8 - 0.6 * math.exp(-0.3 * 0)
QK_SCALE = HEAD_DIM ** -0.5

LANES = 128
BF16_ROWS = 16
VMEM_LIMIT_BYTES = 56 * 1024 * 1024

TOKEN_TILE = 512
ATTN_TQ = 512
ATTN_TK = 512
SUB = 128
GATHER_WIN = SUB + 16
GATHER_TB = 2048
COMBINE_TB = 512
COMBINE_WIN = 256
FFN_TF = 256
NEG_BIG = -1e30

SAFE_RANGE = 40.0
NORM_SLACK = 1.0 + 2.0 ** -7


def _dot(a, b):
    return jnp.dot(a, b, preferred_element_type=F32)


def _cparams(sem):
    return pltpu.CompilerParams(dimension_semantics=sem, vmem_limit_bytes=VMEM_LIMIT_BYTES)


def _split_dot(v, w_bf16):
    hi = v.astype(BF16)
    lo = (v - hi.astype(F32)).astype(BF16)
    return _dot(hi, w_bf16) + _dot(lo, w_bf16)


def _proj_kernel(x_ref, gmix_ref, w_ref, gq_ref, gk_ref, cos_ref, sin_ref, seg_ref,
                 qa_ref, ka_ref, vat_ref, qb_ref, kb_ref, vbt_ref, st_ref):
    tm = x_ref.shape[0]
    x = x_ref[...]
    h = (x * lax.rsqrt(jnp.mean(x * x, axis=-1, keepdims=True) + NORM_EPS) * gmix_ref[...]).astype(BF16)
    cos = cos_ref[...]
    sin = sin_ref[...]
    seg = seg_ref[...]
    lane = lax.broadcasted_iota(I32, (1, LANES), 1)
    first_half = (lane % 32) < 16
    low_head = lane < HEAD_DIM
    ones_lane64 = jnp.where(lane == HEAD_DIM, 1.0, 0.0)
    ones_row = jnp.where(lax.broadcasted_iota(I32, (BF16_ROWS, tm), 0) == 0, 1.0, 0.0)

    def norm_rope(y, gain):
        ss = _split_dot(y * y, seg)
        y = y * lax.rsqrt(ss * (1.0 / HEAD_DIM) + NORM_EPS) * gain
        rot = jnp.where(first_half, pltpu.roll(y, LANES - 16, 1), pltpu.roll(y, 16, 1))
        return y * cos + rot * sin

    def max_head_sumsq(vals):
        sq = vals.astype(F32)
        return jnp.max(_dot((sq * sq).astype(BF16), seg), axis=0, keepdims=True)

    def key_rows(k_pair, odd):
        kh = pltpu.roll(k_pair, HEAD_DIM, 1) if odd else k_pair
        return jnp.where(low_head, kh, ones_lane64).astype(BF16)

    def value_rows_t(vt_rows):
        return jnp.concatenate([vt_rows, ones_row], axis=0).astype(BF16)

    c0 = 0
    aq = _dot(h, w_ref[:, c0:c0 + A_Q_COLS])
    qa_stat = jnp.zeros((1, LANES), F32)
    for c in range(A_Q_COLS // LANES):
        y = (norm_rope(aq[:, c * LANES:(c + 1) * LANES], gq_ref[...]) * QK_SCALE).astype(BF16)
        qa_ref[:, c * LANES:(c + 1) * LANES] = y
        qa_stat = jnp.maximum(qa_stat, max_head_sumsq(y))
    c0 += A_Q_COLS

    akv = _dot(h, w_ref[:, c0:c0 + 2 * A_KV_COLS])
    k = norm_rope(akv[:, :LANES], gk_ref[...]).astype(BF16)
    ka_stat = max_head_sumsq(k)
    k = k.astype(F32)
    vt = akv[:, LANES:].T
    for g in range(A_KV_HEADS):
        ka_ref[g] = key_rows(k, g)
        vat_ref[g] = value_rows_t(vt[g * HEAD_DIM:(g + 1) * HEAD_DIM])
    c0 += 2 * A_KV_COLS

    qb = (_dot(h, w_ref[:, c0:c0 + B_QK_COLS]) * QK_SCALE).astype(BF16)
    qb_ref[...] = qb
    c0 += B_QK_COLS
    bk = _dot(h, w_ref[:, c0:c0 + B_QK_COLS]).astype(BF16)
    c0 += B_QK_COLS
    bv = _dot(h, w_ref[:, c0:c0 + B_V_COLS])
    qb_stat = jnp.zeros((1, LANES), F32)
    kb_stat = jnp.zeros((1, LANES), F32)
    for hd in range(B_HEADS):
        blk = slice(hd * LANES, (hd + 1) * LANES)
        qb_stat = jnp.maximum(qb_stat, max_head_sumsq(qb[:, blk]))
        kb_stat = jnp.maximum(kb_stat, max_head_sumsq(bk[:, blk]))
        k_pair = bk[:, blk].astype(F32)
        kb_ref[hd, 0] = key_rows(k_pair, 0)
        kb_ref[hd, 1] = key_rows(k_pair, 1)
        vbt_ref[hd] = value_rows_t(bv[:, blk].T)

    stats = [jnp.broadcast_to(jnp.max(s, axis=1, keepdims=True), (1, LANES))
             for s in (qa_stat, ka_stat, qb_stat, kb_stat)]
    st_ref[0] = jnp.concatenate(stats + [jnp.zeros((8 - len(stats), LANES), F32)], axis=0)


def _proj(x, gmix, w_in, gq, gk, cos, sin, seg, n):
    ntok, d = x.shape
    tm = min(TOKEN_TILE, n)
    per_seq = n // tm
    ntiles = ntok // tm
    va_rows = HEAD_DIM + BF16_ROWS
    vb_rows = B_V_DIM + BF16_ROWS
    tok = lambda i: (i, 0)
    full = lambda i: (0, 0)
    return pl.pallas_call(
        _proj_kernel,
        grid=(ntiles,),
        in_specs=[
            pl.BlockSpec((tm, d), tok),
            pl.BlockSpec((1, d), full),
            pl.BlockSpec(w_in.shape, full),
            pl.BlockSpec((1, LANES), full),
            pl.BlockSpec((1, LANES), full),
            pl.BlockSpec((tm, LANES), lambda i: (i % per_seq, 0)),
            pl.BlockSpec((tm, LANES), lambda i: (i % per_seq, 0)),
            pl.BlockSpec((LANES, LANES), full),
        ],
        out_specs=[
            pl.BlockSpec((tm, A_Q_COLS), tok),
            pl.BlockSpec((A_KV_HEADS, tm, LANES), lambda i: (0, i, 0)),
            pl.BlockSpec((A_KV_HEADS, va_rows, tm), lambda i: (0, 0, i)),
            pl.BlockSpec((tm, B_QK_COLS), tok),
            pl.BlockSpec((B_HEADS, 2, tm, LANES), lambda i: (0, 0, i, 0)),
            pl.BlockSpec((B_HEADS, vb_rows, tm), lambda i: (0, 0, i)),
            pl.BlockSpec((1, 8, LANES), lambda i: (i, 0, 0)),
        ],
        out_shape=[
            jax.ShapeDtypeStruct((ntok, A_Q_COLS), BF16),
            jax.ShapeDtypeStruct((A_KV_HEADS, ntok, LANES), BF16),
            jax.ShapeDtypeStruct((A_KV_HEADS, va_rows, ntok), BF16),
            jax.ShapeDtypeStruct((ntok, B_QK_COLS), BF16),
            jax.ShapeDtypeStruct((B_HEADS, 2, ntok, LANES), BF16),
            jax.ShapeDtypeStruct((B_HEADS, vb_rows, ntok), BF16),
            jax.ShapeDtypeStruct((ntiles, 8, LANES), F32),
        ],
        compiler_params=_cparams(("arbitrary",)),
        name="proj",
    )(x, gmix, w_in, gq, gk, cos, sin, seg)


def _query_operands(q_pair, kmax, fixed_shift):
    qt = q_pair.astype(F32).T
    tq = qt.shape[1]
    first_row = lax.broadcasted_iota(I32, (HEAD_DIM, tq), 0) == 0
    out = []
    for half in (qt[:HEAD_DIM], qt[HEAD_DIM:]):
        tail = jnp.zeros((HEAD_DIM, tq), F32)
        if fixed_shift:
            shift = jnp.sqrt(jnp.sum(half * half, axis=0, keepdims=True)) * kmax
            tail = jnp.where(first_row, -shift, tail)
        out.append(jnp.concatenate([half, tail], axis=0).astype(BF16))
    return out


def _online_step(k, qt, vt, state, bias=None):
    m, acc = state
    s = _dot(k, qt)
    if bias is not None:
        s = s + bias
    m_new = jnp.maximum(m, jnp.max(s, axis=0, keepdims=True))
    acc = jnp.exp(m - m_new) * acc + _dot(vt, jnp.exp(s - m_new).astype(BF16))
    return m_new, acc


def _two_slot_pipeline(nk, s_ref, score_fn, weigh_fn, accs):
    n_maps = s_ref.shape[1]

    def put(slot, j):
        for h, s in enumerate(score_fn(j)):
            s_ref[slot, h] = s

    def take(slot, j, accs):
        return weigh_fn(j, tuple(s_ref[slot, h] for h in range(n_maps)), accs)

    def body(i, accs):
        put(1, 2 * i + 1)
        accs = take(0, 2 * i, accs)
        put(0, 2 * i + 2)
        return take(1, 2 * i + 1, accs)

    put(0, 0)
    accs = lax.fori_loop(0, nk // 2 - 1, body, accs)
    put(1, nk - 1)
    accs = take(0, nk - 2, accs)
    return take(1, nk - 1, accs)


def _score_scratch(fixed_shift, n_maps, tq, tk, n):
    if not fixed_shift:
        return []
    assert (n // tk) % 2 == 0, "the two-slot score pipeline walks key tiles in pairs"
    return [pltpu.VMEM((2, n_maps, tk, tq), F32)]


def _attn_a_kernel(kmax_ref, q_ref, k_ref, vt_ref, o_ref, *scratch, tk, fixed_shift):
    tq = q_ref.shape[0]
    nk = k_ref.shape[1] // tk
    n_pairs = q_ref.shape[1] // LANES
    kmax = kmax_ref[:, :1]
    qts = []
    for pr in range(n_pairs):
        qts += _query_operands(q_ref[:, pr * LANES:(pr + 1) * LANES], kmax, fixed_shift)

    def tiles(j):
        off = pl.multiple_of(j * tk, tk)
        return k_ref[0, pl.ds(off, tk), :], vt_ref[0, :, pl.ds(off, tk)]

    zero = jnp.zeros((vt_ref.shape[1], tq), F32)
    if fixed_shift:
        def scores(j):
            k = tiles(j)[0]
            return tuple(_dot(k, qt) for qt in qts)

        def weighted(j, ss, accs):
            vt = tiles(j)[1]
            return tuple(acc + _dot(vt, jnp.exp(s).astype(BF16)) for s, acc in zip(ss, accs))

        accs = _two_slot_pipeline(nk, scratch[0], scores, weighted, tuple(zero for _ in qts))
    else:
        def body(j, states):
            k, vt = tiles(j)
            return tuple(_online_step(k, qt, vt, st) for qt, st in zip(qts, states))
        init = (jnp.full((1, tq), NEG_BIG, F32), zero)
        accs = [st[1] for st in lax.fori_loop(0, nk, body, tuple(init for _ in qts))]

    for pr in range(n_pairs):
        halves = [acc[:HEAD_DIM] / acc[HEAD_DIM:HEAD_DIM + 1] for acc in accs[2 * pr:2 * pr + 2]]
        o_ref[:, pr * LANES:(pr + 1) * LANES] = jnp.concatenate(halves, axis=0).T.astype(BF16)


def _attn_a(kmax, qa, ka, vat, *, bsz, n, fixed_shift):
    ntok = qa.shape[0]
    tq = min(ATTN_TQ, n)
    tk = min(ATTN_TK, n)
    nq = n // tq
    gcols = A_Q_COLS // A_KV_HEADS
    return pl.pallas_call(
        functools.partial(_attn_a_kernel, tk=tk, fixed_shift=fixed_shift),
        grid=(bsz, A_KV_HEADS, nq),
        in_specs=[
            pl.BlockSpec((1, LANES), lambda b, g, i: (0, 0)),
            pl.BlockSpec((tq, gcols), lambda b, g, i: (b * nq + i, g)),
            pl.BlockSpec((1, n, LANES), lambda b, g, i: (g, b, 0)),
            pl.BlockSpec((1, vat.shape[1], n), lambda b, g, i: (g, 0, b)),
        ],
        out_specs=pl.BlockSpec((tq, gcols), lambda b, g, i: (b * nq + i, g)),
        out_shape=jax.ShapeDtypeStruct((ntok, A_Q_COLS), BF16),
        scratch_shapes=_score_scratch(fixed_shift, A_Q_HEADS // A_KV_HEADS, tq, tk, n),
        compiler_params=_cparams(("arbitrary", "arbitrary", "arbitrary")),
        name="attn_a_shift" if fixed_shift else "attn_a_online",
    )(kmax, qa, ka, vat)


def _attn_b_kernel(kmax_ref, q_ref, k_ref, vt_ref, bias_ref, lam_ref, gsub_ref, o_ref, *scratch,
                   tk, d0, dstep, fixed_shift):
    tq = q_ref.shape[0]
    nk = k_ref.shape[2] // tk
    n_tiles = bias_ref.shape[1]
    qi = pl.program_id(2)
    qts = _query_operands(q_ref[...], kmax_ref[:, :1], fixed_shift)

    def tiles(j):
        off = pl.multiple_of(j * tk, tk)
        return (k_ref[0, 0, pl.ds(off, tk), :], k_ref[0, 1, pl.ds(off, tk), :]), vt_ref[0, :, pl.ds(off, tk)]

    def bias_tile(j):
        u = jnp.clip(j * tk - qi * tq - d0 + dstep, 0, (n_tiles - 1) * dstep)
        return bias_ref[0, lax.div(u, dstep)]

    zero = jnp.zeros((vt_ref.shape[1], tq), F32)
    if fixed_shift:
        def scores(j):
            return tuple(_dot(k, qt) for k, qt in zip(tiles(j)[0], qts))

        def weighted(j, ss, accs):
            vt = tiles(j)[1]
            bias = bias_tile(j)
            return tuple(acc + _dot(vt, jnp.exp(s + bias).astype(BF16)) for s, acc in zip(ss, accs))

        accs = _two_slot_pipeline(nk, scratch[0], scores, weighted, (zero, zero))
    else:
        def body(j, states):
            ks, vt = tiles(j)
            bias = bias_tile(j)
            return tuple(_online_step(k, qt, vt, st, bias) for k, qt, st in zip(ks, qts, states))
        init = (jnp.full((1, tq), NEG_BIG, F32), zero)
        accs = [st[1] for st in lax.fori_loop(0, nk, body, (init, init))]

    o1, o2 = [acc[:B_V_DIM] / acc[B_V_DIM:B_V_DIM + 1] for acc in accs]
    lp = lam_ref[...]
    lam = (jnp.exp(jnp.sum(lp[0:1] * lp[1:2], axis=-1, keepdims=True))
           - jnp.exp(jnp.sum(lp[2:3] * lp[3:4], axis=-1, keepdims=True)) + LAMBDA_INIT)
    out = (o1 - lam * o2).T
    out = out * lax.rsqrt(jnp.mean(out * out, axis=-1, keepdims=True) + NORM_EPS) * gsub_ref[...]
    o_ref[...] = (out * (1.0 - LAMBDA_INIT)).astype(BF16)


def _t5_bucket(rel):
    nb = REL_BUCKETS // 2
    ret = (rel > 0).astype(I32) * nb
    n = jnp.abs(rel)
    max_exact = nb // 2
    nf = jnp.maximum(n, 1).astype(F32)
    large = max_exact + (jnp.log(nf / max_exact) / math.log(REL_MAX_DIST / max_exact)
                         * (nb - max_exact)).astype(I32)
    large = jnp.minimum(large, nb - 1)
    return ret + jnp.where(n < max_exact, n, large)


def _bias_tiles(rel_bias, tq, tk):
    dstep = math.gcd(tq, tk)
    reach = (tq + tk + REL_MAX_DIST) // dstep
    band = [d for d in range(-reach * dstep, reach * dstep + 1, dstep)
            if d + tk - 1 > -REL_MAX_DIST and d - (tq - 1) < REL_MAX_DIST]
    far = tq + tk + REL_MAX_DIST
    ds = jnp.asarray([-far] + band + [far], I32)
    rel = (ds[:, None, None] + jnp.arange(tk, dtype=I32)[None, :, None]
           - jnp.arange(tq, dtype=I32)[None, None, :])
    bucket = _t5_bucket(rel)
    table = rel_bias.astype(F32)
    table = table - jnp.max(table, axis=0, keepdims=True)
    tiles = jnp.zeros((table.shape[1],) + bucket.shape, F32)
    for b in range(REL_BUCKETS):
        tiles = jnp.where(bucket[None] == b, table[b][:, None, None, None], tiles)
    return tiles, band[0], dstep


def _attn_b(kmax, qb, kb, vbt, tiles, lam_params, gsub, *, bsz, n, d0, dstep, fixed_shift):
    ntok = qb.shape[0]
    tk, tq = tiles.shape[2], tiles.shape[3]
    nq = n // tq
    return pl.pallas_call(
        functools.partial(_attn_b_kernel, tk=tk, d0=d0, dstep=dstep, fixed_shift=fixed_shift),
        grid=(bsz, B_HEADS, nq),
        in_specs=[
            pl.BlockSpec((1, LANES), lambda b, h, i: (0, 0)),
            pl.BlockSpec((tq, LANES), lambda b, h, i: (b * nq + i, h)),
            pl.BlockSpec((1, 2, n, LANES), lambda b, h, i: (h, 0, b, 0)),
            pl.BlockSpec((1, vbt.shape[1], n), lambda b, h, i: (h, 0, b)),
            pl.BlockSpec((1,) + tiles.shape[1:], lambda b, h, i: (h, 0, 0, 0)),
            pl.BlockSpec(lam_params.shape, lambda b, h, i: (0, 0)),
            pl.BlockSpec((1, LANES), lambda b, h, i: (0, 0)),
        ],
        out_specs=pl.BlockSpec((tq, LANES), lambda b, h, i: (b * nq + i, h)),
        out_shape=jax.ShapeDtypeStruct((ntok, B_V_COLS), BF16),
        scratch_shapes=_score_scratch(fixed_shift, 2, tq, tk, n),
        compiler_params=_cparams(("arbitrary", "arbitrary", "arbitrary")),
        name="attn_b_shift" if fixed_shift else "attn_b_online",
    )(kmax, qb, kb, vbt, tiles, lam_params, gsub)


def _outproj_kernel(x_ref, a_ref, b_ref, wo_ref, gffn_ref, wr_hi_ref, wr_lo_ref,
                    x1_ref, h2_ref, afft_ref, *, n_experts):
    wa = wo_ref.shape[0] // 2
    x1 = x_ref[...] + _dot(a_ref[...], wo_ref[:wa]) + _dot(b_ref[...], wo_ref[wa:])
    x1_ref[...] = x1
    h2 = x1 * lax.rsqrt(jnp.mean(x1 * x1, axis=-1, keepdims=True) + NORM_EPS) * gffn_ref[...]
    h2_ref[...] = h2.astype(BF16)
    hi = h2.astype(BF16)
    lo = (h2 - hi.astype(F32)).astype(BF16)
    logits = _dot(hi, wr_hi_ref[...]) + _dot(hi, wr_lo_ref[...]) + _dot(lo, wr_hi_ref[...])
    lane = lax.broadcasted_iota(I32, (1, LANES), 1)
    logits = jnp.where(lane < n_experts, logits, NEG_BIG)
    ex = jnp.exp(logits - jnp.max(logits, axis=-1, keepdims=True))
    aff = ex / jnp.sum(ex, axis=-1, keepdims=True)
    afft_ref[...] = aff.T[:n_experts]


def _outproj(x, a_out, b_out, w_out, gffn, wr_hi, wr_lo, n_experts):
    ntok, d = x.shape
    tm = min(TOKEN_TILE, ntok)
    tok = lambda i: (i, 0)
    full = lambda i: (0, 0)
    return pl.pallas_call(
        functools.partial(_outproj_kernel, n_experts=n_experts),
        grid=(ntok // tm,),
        in_specs=[
            pl.BlockSpec((tm, d), tok),
            pl.BlockSpec((tm, a_out.shape[1]), tok),
            pl.BlockSpec((tm, b_out.shape[1]), tok),
            pl.BlockSpec(w_out.shape, full),
            pl.BlockSpec((1, d), full),
            pl.BlockSpec(wr_hi.shape, full),
            pl.BlockSpec(wr_lo.shape, full),
        ],
        out_specs=[
            pl.BlockSpec((tm, d), tok),
            pl.BlockSpec((tm, d), tok),
            pl.BlockSpec((n_experts, tm), lambda i: (0, i)),
        ],
        out_shape=[
            jax.ShapeDtypeStruct((ntok, d), F32),
            jax.ShapeDtypeStruct((ntok, d), BF16),
            jax.ShapeDtypeStruct((n_experts, ntok), F32),
        ],
        compiler_params=_cparams(("arbitrary",)),
        name="outproj",
    )(x, a_out, b_out, w_out, gffn, wr_hi, wr_lo)


def _topk_kernel(aff_ref, slot_ref, gsel_ref, cstart_ref, *, cap):
    aff = aff_ref[...]
    n_exp, ntok = aff.shape
    bits = lax.bitcast_convert_type(aff, I32)

    def count(mask):
        return jnp.sum(mask.astype(I32), axis=1, keepdims=True)

    def value_bit(i, t):
        cand = t | lax.shift_left(jnp.int32(1), 30 - i)
        return jnp.where(count(bits >= cand) >= cap, cand, t)

    thr = lax.fori_loop(0, 31, value_bit, jnp.zeros((n_exp, 1), I32))
    above = bits > thr
    tie = bits == thr
    need = cap - count(above)
    idx = lax.broadcasted_iota(I32, aff.shape, 1)

    n_bits = int(ntok).bit_length()

    def index_bit(i, bound):
        cand = bound + lax.shift_left(jnp.int32(1), n_bits - 1 - i)
        ok = (cand <= ntok) & (count(tie & (idx < cand)) <= need)
        return jnp.where(ok, cand, bound)

    bound = lax.fori_loop(0, n_bits, index_bit, jnp.zeros((n_exp, 1), I32))
    sel = above | (tie & (idx < bound))
    gsel_ref[...] = jnp.where(sel, aff, 0.0)

    r = lax.broadcasted_iota(I32, (SUB, SUB), 0)
    c = lax.broadcasted_iota(I32, (SUB, SUB), 1)
    before = jnp.where(r < c, 1.0, 0.0).astype(BF16)
    carry = jnp.zeros((n_exp, 1), F32)
    n_sub = ntok // SUB
    for s in range(n_sub):
        sc = sel[:, s * SUB:(s + 1) * SUB]
        scf = jnp.where(sc, 1.0, 0.0)
        pos = (_dot(scf.astype(BF16), before) + carry).astype(I32)
        slot_ref[:, s * SUB:(s + 1) * SUB] = jnp.where(sc, pos, -1)
        cstart_ref[:, s:s + 1] = carry.astype(I32)
        carry = carry + jnp.sum(scf, axis=1, keepdims=True)
    cstart_ref[:, n_sub:n_sub + 1] = carry.astype(I32)


def _topk(afft, cap):
    n_exp, ntok = afft.shape
    nsub = ntok // SUB
    return pl.pallas_call(
        functools.partial(_topk_kernel, cap=cap),
        out_shape=[
            jax.ShapeDtypeStruct((n_exp, ntok), I32),
            jax.ShapeDtypeStruct((n_exp, ntok), F32),
            jax.ShapeDtypeStruct((n_exp, nsub + 1), I32),
        ],
        compiler_params=pltpu.CompilerParams(vmem_limit_bytes=VMEM_LIMIT_BYTES),
        name="topk",
    )(afft)


def _gather_kernel(cstart_ref, slot_ref, gsel_ref, h_ref, xe_ref, gs_ref, acc_ref, gacc_ref, *, cap):
    e = pl.program_id(0)
    i = pl.program_id(1)
    n_sub = h_ref.shape[0] // SUB

    @pl.when(i == 0)
    def _():
        acc_ref[...] = jnp.zeros_like(acc_ref)
        gacc_ref[...] = jnp.zeros_like(gacc_ref)

    row = lax.broadcasted_iota(I32, (GATHER_WIN, SUB), 0)
    for s in range(n_sub):
        cs = cstart_ref[e, i * n_sub + s]
        base = pl.multiple_of(lax.div(cs, 16) * 16, 16)
        sl = slot_ref[0, :, s * SUB:(s + 1) * SUB]
        hit = row == (sl - base)
        onehot = jnp.where(hit, 1.0, 0.0).astype(BF16)
        acc_ref[pl.ds(base, GATHER_WIN), :] += _dot(onehot, h_ref[s * SUB:(s + 1) * SUB, :])
        g = gsel_ref[0, :, s * SUB:(s + 1) * SUB]
        gw = jnp.sum(jnp.where(hit, g, 0.0), axis=1, keepdims=True)
        gacc_ref[pl.ds(base, GATHER_WIN), :] += jnp.broadcast_to(gw, (GATHER_WIN, LANES))

    @pl.when(i == pl.num_programs(1) - 1)
    def _():
        xe_ref[0] = acc_ref[:cap].astype(BF16)
        gs_ref[0] = gacc_ref[:cap]


def _gather(cstart, slot3, gsel3, h2, cap):
    n_exp = slot3.shape[0]
    ntok, d = h2.shape
    tb = min(GATHER_TB, ntok)
    return pl.pallas_call(
        functools.partial(_gather_kernel, cap=cap),
        grid_spec=pltpu.PrefetchScalarGridSpec(
            num_scalar_prefetch=1,
            grid=(n_exp, ntok // tb),
            in_specs=[
                pl.BlockSpec((1, 1, tb), lambda e, i, cs: (e, 0, i)),
                pl.BlockSpec((1, 1, tb), lambda e, i, cs: (e, 0, i)),
                pl.BlockSpec((tb, d), lambda e, i, cs: (i, 0)),
            ],
            out_specs=[
                pl.BlockSpec((1, cap, d), lambda e, i, cs: (e, 0, 0)),
                pl.BlockSpec((1, cap, LANES), lambda e, i, cs: (e, 0, 0)),
            ],
            scratch_shapes=[
                pltpu.VMEM((cap + GATHER_WIN, d), F32),
                pltpu.VMEM((cap + GATHER_WIN, LANES), F32),
            ],
        ),
        out_shape=[
            jax.ShapeDtypeStruct((n_exp, cap, d), BF16),
            jax.ShapeDtypeStruct((n_exp, cap, LANES), F32),
        ],
        compiler_params=_cparams(("arbitrary", "arbitrary")),
        name="gather",
    )(cstart, slot3, gsel3, h2)


def _ffn_kernel(xe_ref, gs_ref, wg_ref, wu_ref, wd_ref, ye_ref, acc_ref):
    f = pl.program_id(1)
    x = xe_ref[0]
    g = _dot(x, wg_ref[0].astype(BF16))
    u = _dot(x, wu_ref[0].astype(BF16))
    hid = (g * jax.nn.sigmoid(g) * u).astype(BF16)
    contrib = _dot(hid, wd_ref[0].astype(BF16))

    @pl.when(f == 0)
    def _():
        acc_ref[...] = contrib

    @pl.when(f > 0)
    def _():
        acc_ref[...] += contrib

    @pl.when(f == pl.num_programs(1) - 1)
    def _():
        ye_ref[0] = (acc_ref[...] * gs_ref[0][:, :1]).astype(BF16)


def _ffn(xe, gs, w_gate, w_up, w_down):
    n_exp, cap, d = xe.shape
    ff = w_gate.shape[-1]
    tf = min(FFN_TF, ff)
    return pl.pallas_call(
        _ffn_kernel,
        grid=(n_exp, ff // tf),
        in_specs=[
            pl.BlockSpec((1, cap, d), lambda e, f: (e, 0, 0)),
            pl.BlockSpec((1, cap, LANES), lambda e, f: (e, 0, 0)),
            pl.BlockSpec((1, d, tf), lambda e, f: (e, 0, f)),
            pl.BlockSpec((1, d, tf), lambda e, f: (e, 0, f)),
            pl.BlockSpec((1, tf, d), lambda e, f: (e, f, 0)),
        ],
        out_specs=pl.BlockSpec((1, cap, d), lambda e, f: (e, 0, 0)),
        out_shape=jax.ShapeDtypeStruct((n_exp, cap, d), BF16),
        scratch_shapes=[pltpu.VMEM((cap, d), F32)],
        compiler_params=_cparams(("arbitrary", "arbitrary")),
        name="ffn",
    )(xe, gs, w_gate, w_up, w_down)


def _combine_kernel(cstart_ref, x1_ref, slot_ref, ya_ref, yb_ref, gfin_ref, y_ref, acc_ref, *, rows, nblk):
    i = pl.program_id(0)
    e = pl.program_id(1)
    n_sub = x1_ref.shape[0] // SUB
    blk_a = jnp.minimum(lax.div(cstart_ref[e, i * n_sub], rows), nblk - 1)
    split = (blk_a + 1) * rows

    @pl.when(e == 0)
    def _():
        acc_ref[...] = x1_ref[...]

    row = lax.broadcasted_iota(I32, (COMBINE_WIN, SUB), 0)
    dims = (((0,), (0,)), ((), ()))
    for s in range(n_sub):
        cs = cstart_ref[e, i * n_sub + s]
        cs_end = cstart_ref[e, i * n_sub + s + 1]
        r0 = lax.div(cs, 16) * 16 - blk_a * rows

        def scatter(which, y_blk, s=s, r0=r0):
            start = pl.multiple_of(jnp.clip(r0 - which * rows, 0, rows - COMBINE_WIN), 16)
            first_slot = (blk_a + which) * rows + start
            sl = slot_ref[0, :, s * SUB:(s + 1) * SUB]
            onehot = jnp.where(row == (sl - first_slot), 1.0, 0.0).astype(BF16)
            win = y_blk[0, pl.ds(start, COMBINE_WIN), :]
            acc_ref[s * SUB:(s + 1) * SUB, :] += lax.dot_general(onehot, win, dims,
                                                                 preferred_element_type=F32)

        pl.when((cs_end > cs) & (cs < split))(functools.partial(scatter, 0, ya_ref))
        pl.when((cs_end > cs) & (cs_end > split))(functools.partial(scatter, 1, yb_ref))

    @pl.when(e == pl.num_programs(1) - 1)
    def _():
        out = acc_ref[...]
        y_ref[...] = out * lax.rsqrt(jnp.mean(out * out, axis=-1, keepdims=True) + NORM_EPS) * gfin_ref[...]


def _combine(cstart, x1, slot3, ye, gfin):
    n_exp, cap, d = ye.shape
    ntok = x1.shape[0]
    tb = min(COMBINE_TB, ntok, cap)
    rows = tb
    n_sub = tb // SUB
    nblk = cap // rows

    def blk(e, i, cs, which):
        a = jnp.minimum(cs[e, i * n_sub] // rows, nblk - 1)
        return jnp.minimum(a + which, nblk - 1)

    return pl.pallas_call(
        functools.partial(_combine_kernel, rows=rows, nblk=nblk),
        grid_spec=pltpu.PrefetchScalarGridSpec(
            num_scalar_prefetch=1,
            grid=(ntok // tb, n_exp),
            in_specs=[
                pl.BlockSpec((tb, d), lambda i, e, cs: (i, 0)),
                pl.BlockSpec((1, 1, tb), lambda i, e, cs: (e, 0, i)),
                pl.BlockSpec((1, rows, d), lambda i, e, cs: (e, blk(e, i, cs, 0), 0)),
                pl.BlockSpec((1, rows, d), lambda i, e, cs: (e, blk(e, i, cs, 1), 0)),
                pl.BlockSpec((1, d), lambda i, e, cs: (0, 0)),
            ],
            out_specs=pl.BlockSpec((tb, d), lambda i, e, cs: (i, 0)),
            scratch_shapes=[pltpu.VMEM((tb, d), F32)],
        ),
        out_shape=jax.ShapeDtypeStruct((ntok, d), F32),
        compiler_params=_cparams(("arbitrary", "arbitrary")),
        name="combine",
    )(cstart, x1, slot3, ye, ye, gfin)


def _rope_tables(n):
    rows = n // GRID_W
    row = jnp.broadcast_to(jnp.arange(rows)[:, None], (rows, GRID_W)).reshape(-1).astype(F32)
    col = jnp.broadcast_to(jnp.arange(GRID_W)[None, :], (rows, GRID_W)).reshape(-1).astype(F32)
    half = HEAD_DIM // 2
    inv = ROPE_THETA ** (-jnp.arange(0, half, 2, dtype=F32) / half)
    ang_r = row[:, None] * inv[None, :]
    ang_c = col[:, None] * inv[None, :]
    cos = jnp.concatenate([jnp.cos(ang_r), jnp.cos(ang_r), jnp.cos(ang_c), jnp.cos(ang_c)], axis=-1)
    sin = jnp.concatenate([-jnp.sin(ang_r), jnp.sin(ang_r), -jnp.sin(ang_c), jnp.sin(ang_c)], axis=-1)
    return jnp.tile(cos, (1, 2)), jnp.tile(sin, (1, 2))


def _encoder(x, p):
    bsz, n, d = x.shape
    ntok = bsz * n
    n_exp = p["n_exp"]
    cap = CAPACITY_FACTOR * ntok // n_exp
    xt = x.reshape(ntok, d)
    cos, sin = _rope_tables(n)
    qa, ka, vat, qb, kb, vbt, stats = _proj(xt, p["gmix"], p["w_in"], p["gq"], p["gk"], cos, sin,
                                            p["seg"], n)

    qa_n, ka_n, qb_n, kb_n = jnp.sqrt(jnp.max(stats[:, :4, 0], axis=0)) * NORM_SLACK
    lanes = lambda v: jnp.broadcast_to(v, (1, LANES)).astype(F32)
    a_args = (lanes(ka_n), qa, ka, vat)
    a_out = lax.cond(qa_n * ka_n <= SAFE_RANGE,
                     functools.partial(_attn_a, bsz=bsz, n=n, fixed_shift=True),
                     functools.partial(_attn_a, bsz=bsz, n=n, fixed_shift=False), *a_args)
    tiles, d0, dstep = p["bias_tiles"](min(ATTN_TQ, n), min(ATTN_TK, n))
    b_args = (lanes(kb_n), qb, kb, vbt, tiles, p["lam"], p["gsub"])
    b_kw = dict(bsz=bsz, n=n, d0=d0, dstep=dstep)
    b_out = lax.cond(2.0 * qb_n * kb_n + p["bias_spread"] <= 2.0 * SAFE_RANGE,
                     functools.partial(_attn_b, fixed_shift=True, **b_kw),
                     functools.partial(_attn_b, fixed_shift=False, **b_kw), *b_args)

    x1, h2, afft = _outproj(xt, a_out, b_out, p["w_out"], p["gffn"], p["wr_hi"], p["wr_lo"], n_exp)
    slot, gsel, cstart = _topk(afft, cap)
    slot3 = slot.reshape(n_exp, 1, ntok)
    gsel3 = gsel.reshape(n_exp, 1, ntok)
    xe, gs = _gather(cstart, slot3, gsel3, h2, cap)
    ye = _ffn(xe, gs, p["w_gate"], p["w_up"], p["w_down"])
    y = _combine(cstart, x1, slot3, ye, p["gfin"])
    return y.reshape(bsz, n, d)


def kernel(x_prompt, x_sample, norm_mix, w_in, a_q_norm, a_k_norm, b_lambda, b_subln, w_out, rel_bias,
           norm_ffn, w_router, w_gate, w_up, w_down, norm_final):
    d = x_prompt.shape[-1]
    n_exp = w_router.shape[-1]
    wr = jnp.zeros((d, LANES), F32).at[:, :n_exp].set(w_router[0].astype(F32))
    wr_hi = wr.astype(BF16)
    lane_head = np.arange(LANES) // HEAD_DIM
    tile_cache = {}

    def bias_tiles(tq, tk):
        if (tq, tk) not in tile_cache:
            tile_cache[(tq, tk)] = _bias_tiles(rel_bias, tq, tk)
        return tile_cache[(tq, tk)]

    p = dict(
        n_exp=n_exp,
        gmix=norm_mix[0].astype(F32).reshape(1, d),
        w_in=w_in[0].astype(BF16),
        gq=jnp.tile(a_q_norm[0].astype(F32), 2).reshape(1, LANES),
        gk=jnp.tile(a_k_norm[0].astype(F32), 2).reshape(1, LANES),
        seg=jnp.asarray(lane_head[:, None] == lane_head[None, :], BF16),
        bias_tiles=bias_tiles,
        bias_spread=(jnp.max(rel_bias) - jnp.min(rel_bias)).astype(F32),
        lam=b_lambda[0].astype(F32),
        gsub=b_subln[0].astype(F32).reshape(1, LANES),
        w_out=w_out[0].astype(BF16),
        gffn=norm_ffn[0].astype(F32).reshape(1, d),
        wr_hi=wr_hi,
        wr_lo=(wr - wr_hi.astype(F32)).astype(BF16),
        w_gate=w_gate[0], w_up=w_up[0], w_down=w_down[0],
        gfin=norm_final.astype(F32).reshape(1, d),
    )
    return (_encoder(x_prompt, p), _encoder(x_sample, p))
```

```python
import functools
import math

import jax
import jax.numpy as jnp
import numpy as np
from jax import lax
from jax.experimental import pallas as pl
from jax.experimental.pallas import tpu as pltpu

F32 = jnp.float32
BF16 = jnp.bfloat16
I32 = jnp.int32

HEAD_DIM = 64
A_Q_HEADS = 8
A_KV_HEADS = 2
B_HEADS = 4
A_Q_COLS = A_Q_HEADS * HEAD_DIM
A_KV_COLS = A_KV_HEADS * HEAD_DIM
B_QK_COLS = B_HEADS * 2 * HEAD_DIM
B_V_DIM = 2 * HEAD_DIM
B_V_COLS = B_HEADS * B_V_DIM
GRID_W = 64
ROPE_THETA = 10000.0
REL_BUCKETS = 32
REL_MAX_DIST = 128
CAPACITY_FACTOR = 2
NORM_EPS = 1e-6
LAMBDA_INIT = 0.8 - 0.6 * math.exp(-0.3 * 0)
QK_SCALE = HEAD_DIM ** -0.5

LANES = 128
BF16_ROWS = 16
VMEM_LIMIT_BYTES = 56 * 1024 * 1024

TOKEN_TILE = 512
ATTN_TQ = 512
ATTN_TK = 512
TILES_PER_BODY = 4
SUB = 128
GATHER_WIN = SUB + 16
GATHER_TB = 2048
COMBINE_TB = 512
COMBINE_WIN = 256
FFN_TF = 256
FFN_ROWS = 512
NEG_BIG = -1e30

SAFE_RANGE = 40.0
NORM_SLACK = 1.0 + 2.0 ** -7


def _dot(a, b):
    return jnp.dot(a, b, preferred_element_type=F32)


def _cparams(sem):
    return pltpu.CompilerParams(dimension_semantics=sem, vmem_limit_bytes=VMEM_LIMIT_BYTES)


def _split_dot(v, w_bf16):
    hi = v.astype(BF16)
    lo = (v - hi.astype(F32)).astype(BF16)
    return _dot(hi, w_bf16) + _dot(lo, w_bf16)


def _proj_kernel(x_ref, gmix_ref, w_ref, gq_ref, gk_ref, cos_ref, sin_ref, seg_ref,
                 qa_ref, ka_ref, vat_ref, qb_ref, kb_ref, vbt_ref, st_ref):
    tm = x_ref.shape[0]
    x = x_ref[...]
    h = (x * lax.rsqrt(jnp.mean(x * x, axis=-1, keepdims=True) + NORM_EPS) * gmix_ref[...]).astype(BF16)
    cos = cos_ref[...]
    sin = sin_ref[...]
    seg = seg_ref[...]
    lane = lax.broadcasted_iota(I32, (1, LANES), 1)
    first_half = (lane % 32) < 16
    low_head = lane < HEAD_DIM
    ones_lane64 = jnp.where(lane == HEAD_DIM, 1.0, 0.0)
    ones_row = jnp.where(lax.broadcasted_iota(I32, (BF16_ROWS, tm), 0) == 0, 1.0, 0.0)

    def norm_rope(y, gain):
        ss = _split_dot(y * y, seg)
        y = y * lax.rsqrt(ss * (1.0 / HEAD_DIM) + NORM_EPS) * gain
        rot = jnp.where(first_half, pltpu.roll(y, LANES - 16, 1), pltpu.roll(y, 16, 1))
        return y * cos + rot * sin

    def max_head_sumsq(vals):
        sq = vals.astype(F32)
        return jnp.max(_dot((sq * sq).astype(BF16), seg), axis=0, keepdims=True)

    def key_rows(k_pair, odd):
        kh = pltpu.roll(k_pair, HEAD_DIM, 1) if odd else k_pair
        return jnp.where(low_head, kh, ones_lane64).astype(BF16)

    def value_rows_t(vt_rows):
        return jnp.concatenate([vt_rows, ones_row], axis=0).astype(BF16)

    c0 = 0
    aq = _dot(h, w_ref[:, c0:c0 + A_Q_COLS])
    qa_stat = jnp.zeros((1, LANES), F32)
    for c in range(A_Q_COLS // LANES):
        y = (norm_rope(aq[:, c * LANES:(c + 1) * LANES], gq_ref[...]) * QK_SCALE).astype(BF16)
        qa_ref[:, c * LANES:(c + 1) * LANES] = y
        qa_stat = jnp.maximum(qa_stat, max_head_sumsq(y))
    c0 += A_Q_COLS

    akv = _dot(h, w_ref[:, c0:c0 + 2 * A_KV_COLS])
    k = norm_rope(akv[:, :LANES], gk_ref[...]).astype(BF16)
    ka_stat = max_head_sumsq(k)
    k = k.astype(F32)
    vt = akv[:, LANES:].T
    for g in range(A_KV_HEADS):
        ka_ref[g] = key_rows(k, g)
        vat_ref[g] = value_rows_t(vt[g * HEAD_DIM:(g + 1) * HEAD_DIM])
    c0 += 2 * A_KV_COLS

    qb = (_dot(h, w_ref[:, c0:c0 + B_QK_COLS]) * QK_SCALE).astype(BF16)
    qb_ref[...] = qb
    c0 += B_QK_COLS
    bk = _dot(h, w_ref[:, c0:c0 + B_QK_COLS]).astype(BF16)
    c0 += B_QK_COLS
    bv = _dot(h, w_ref[:, c0:c0 + B_V_COLS])
    qb_stat = jnp.zeros((1, LANES), F32)
    kb_stat = jnp.zeros((1, LANES), F32)
    for hd in range(B_HEADS):
        blk = slice(hd * LANES, (hd + 1) * LANES)
        qb_stat = jnp.maximum(qb_stat, max_head_sumsq(qb[:, blk]))
        kb_stat = jnp.maximum(kb_stat, max_head_sumsq(bk[:, blk]))
        k_pair = bk[:, blk].astype(F32)
        kb_ref[hd, 0] = key_rows(k_pair, 0)
        kb_ref[hd, 1] = key_rows(k_pair, 1)
        vbt_ref[hd] = value_rows_t(bv[:, blk].T)

    stats = [jnp.broadcast_to(jnp.max(s, axis=1, keepdims=True), (1, LANES))
             for s in (qa_stat, ka_stat, qb_stat, kb_stat)]
    st_ref[0] = jnp.concatenate(stats + [jnp.zeros((8 - len(stats), LANES), F32)], axis=0)


def _proj(x, gmix, w_in, gq, gk, cos, sin, seg, n):
    ntok, d = x.shape
    tm = min(TOKEN_TILE, n)
    per_seq = n // tm
    ntiles = ntok // tm
    va_rows = HEAD_DIM + BF16_ROWS
    vb_rows = B_V_DIM + BF16_ROWS
    tok = lambda i: (i, 0)
    full = lambda i: (0, 0)
    return pl.pallas_call(
        _proj_kernel,
        grid=(ntiles,),
        in_specs=[
            pl.BlockSpec((tm, d), tok),
            pl.BlockSpec((1, d), full),
            pl.BlockSpec(w_in.shape, full),
            pl.BlockSpec((1, LANES), full),
            pl.BlockSpec((1, LANES), full),
            pl.BlockSpec((tm, LANES), lambda i: (i % per_seq, 0)),
            pl.BlockSpec((tm, LANES), lambda i: (i % per_seq, 0)),
            pl.BlockSpec((LANES, LANES), full),
        ],
        out_specs=[
            pl.BlockSpec((tm, A_Q_COLS), tok),
            pl.BlockSpec((A_KV_HEADS, tm, LANES), lambda i: (0, i, 0)),
            pl.BlockSpec((A_KV_HEADS, va_rows, tm), lambda i: (0, 0, i)),
            pl.BlockSpec((tm, B_QK_COLS), tok),
            pl.BlockSpec((B_HEADS, 2, tm, LANES), lambda i: (0, 0, i, 0)),
            pl.BlockSpec((B_HEADS, vb_rows, tm), lambda i: (0, 0, i)),
            pl.BlockSpec((1, 8, LANES), lambda i: (i, 0, 0)),
        ],
        out_shape=[
            jax.ShapeDtypeStruct((ntok, A_Q_COLS), BF16),
            jax.ShapeDtypeStruct((A_KV_HEADS, ntok, LANES), BF16),
            jax.ShapeDtypeStruct((A_KV_HEADS, va_rows, ntok), BF16),
            jax.ShapeDtypeStruct((ntok, B_QK_COLS), BF16),
            jax.ShapeDtypeStruct((B_HEADS, 2, ntok, LANES), BF16),
            jax.ShapeDtypeStruct((B_HEADS, vb_rows, ntok), BF16),
            jax.ShapeDtypeStruct((ntiles, 8, LANES), F32),
        ],
        compiler_params=_cparams(("arbitrary",)),
        name="proj",
    )(x, gmix, w_in, gq, gk, cos, sin, seg)


def _query_operands(q_pair, kmax, fixed_shift):
    qt = q_pair.astype(F32).T
    tq = qt.shape[1]
    first_row = lax.broadcasted_iota(I32, (HEAD_DIM, tq), 0) == 0
    out = []
    for half in (qt[:HEAD_DIM], qt[HEAD_DIM:]):
        tail = jnp.zeros((HEAD_DIM, tq), F32)
        if fixed_shift:
            shift = jnp.sqrt(jnp.sum(half * half, axis=0, keepdims=True)) * kmax
            tail = jnp.where(first_row, -shift, tail)
        out.append(jnp.concatenate([half, tail], axis=0).astype(BF16))
    return out


def _online_step(k, qt, vt, state, bias=None):
    m, acc = state
    s = _dot(k, qt)
    if bias is not None:
        s = s + bias
    m_new = jnp.maximum(m, jnp.max(s, axis=0, keepdims=True))
    acc = jnp.exp(m - m_new) * acc + _dot(vt, jnp.exp(s - m_new).astype(BF16))
    return m_new, acc


def _two_slot_pipeline(nk, s_ref, score_fn, weigh_fn, accs):
    n_maps = s_ref.shape[1]

    def put(slot, j):
        for h, s in enumerate(score_fn(j)):
            s_ref[slot, h] = s

    def take(slot, j, accs):
        return weigh_fn(j, tuple(s_ref[slot, h] for h in range(n_maps)), accs)

    def run(first, count, accs):
        for t in range(count):
            put((t + 1) % 2, first + t + 1)
            accs = take(t % 2, first + t, accs)
        return accs

    put(0, 0)
    accs = lax.fori_loop(0, nk // TILES_PER_BODY - 1,
                         lambda i, accs: run(TILES_PER_BODY * i, TILES_PER_BODY, accs), accs)
    accs = run(nk - TILES_PER_BODY, TILES_PER_BODY - 1, accs)
    return take((TILES_PER_BODY - 1) % 2, nk - 1, accs)


def _score_scratch(fixed_shift, n_maps, tq, tk, n):
    if not fixed_shift:
        return []
    assert (n // tk) % TILES_PER_BODY == 0 and TILES_PER_BODY % 2 == 0
    return [pltpu.VMEM((2, n_maps, tk, tq), F32)]


def _attn_a_kernel(kmax_ref, q_ref, k_ref, vt_ref, o_ref, *scratch, tk, fixed_shift):
    tq = q_ref.shape[0]
    nk = k_ref.shape[1] // tk
    n_pairs = q_ref.shape[1] // LANES
    kmax = kmax_ref[:, :1]
    qts = []
    for pr in range(n_pairs):
        qts += _query_operands(q_ref[:, pr * LANES:(pr + 1) * LANES], kmax, fixed_shift)

    def tiles(j):
        off = pl.multiple_of(j * tk, tk)
        return k_ref[0, pl.ds(off, tk), :], vt_ref[0, :, pl.ds(off, tk)]

    zero = jnp.zeros((vt_ref.shape[1], tq), F32)
    if fixed_shift:
        def scores(j):
            k = tiles(j)[0]
            return tuple(_dot(k, qt) for qt in qts)

        def weighted(j, ss, accs):
            vt = tiles(j)[1]
            return tuple(acc + _dot(vt, jnp.exp(s).astype(BF16)) for s, acc in zip(ss, accs))

        accs = _two_slot_pipeline(nk, scratch[0], scores, weighted, tuple(zero for _ in qts))
    else:
        def body(j, states):
            k, vt = tiles(j)
            return tuple(_online_step(k, qt, vt, st) for qt, st in zip(qts, states))
        init = (jnp.full((1, tq), NEG_BIG, F32), zero)
        accs = [st[1] for st in lax.fori_loop(0, nk, body, tuple(init for _ in qts))]

    for pr in range(n_pairs):
        halves = [acc[:HEAD_DIM] / acc[HEAD_DIM:HEAD_DIM + 1] for acc in accs[2 * pr:2 * pr + 2]]
        o_ref[:, pr * LANES:(pr + 1) * LANES] = jnp.concatenate(halves, axis=0).T.astype(BF16)


def _attn_a(kmax, qa, ka, vat, *, bsz, n, fixed_shift):
    ntok = qa.shape[0]
    tq = min(ATTN_TQ, n)
    tk = min(ATTN_TK, n)
    nq = n // tq
    gcols = A_Q_COLS // A_KV_HEADS
    return pl.pallas_call(
        functools.partial(_attn_a_kernel, tk=tk, fixed_shift=fixed_shift),
        grid=(bsz, A_KV_HEADS, nq),
        in_specs=[
            pl.BlockSpec((1, LANES), lambda b, g, i: (0, 0)),
            pl.BlockSpec((tq, gcols), lambda b, g, i: (b * nq + i, g)),
            pl.BlockSpec((1, n, LANES), lambda b, g, i: (g, b, 0)),
            pl.BlockSpec((1, vat.shape[1], n), lambda b, g, i: (g, 0, b)),
        ],
        out_specs=pl.BlockSpec((tq, gcols), lambda b, g, i: (b * nq + i, g)),
        out_shape=jax.ShapeDtypeStruct((ntok, A_Q_COLS), BF16),
        scratch_shapes=_score_scratch(fixed_shift, A_Q_HEADS // A_KV_HEADS, tq, tk, n),
        compiler_params=_cparams(("arbitrary", "arbitrary", "arbitrary")),
        name="attn_a_shift" if fixed_shift else "attn_a_online",
    )(kmax, qa, ka, vat)


def _attn_b_kernel(kmax_ref, q_ref, k_ref, vt_ref, bias_ref, lam_ref, gsub_ref, o_ref, *scratch,
                   tk, d0, dstep, fixed_shift):
    tq = q_ref.shape[0]
    nk = k_ref.shape[2] // tk
    n_tiles = bias_ref.shape[1]
    qi = pl.program_id(2)
    qts = _query_operands(q_ref[...], kmax_ref[:, :1], fixed_shift)

    def tiles(j):
        off = pl.multiple_of(j * tk, tk)
        return (k_ref[0, 0, pl.ds(off, tk), :], k_ref[0, 1, pl.ds(off, tk), :]), vt_ref[0, :, pl.ds(off, tk)]

    def bias_tile(j):
        u = jnp.clip(j * tk - qi * tq - d0 + dstep, 0, (n_tiles - 1) * dstep)
        return bias_ref[0, lax.div(u, dstep)]

    zero = jnp.zeros((vt_ref.shape[1], tq), F32)
    if fixed_shift:
        def scores(j):
            return tuple(_dot(k, qt) for k, qt in zip(tiles(j)[0], qts))

        def weighted(j, ss, accs):
            vt = tiles(j)[1]
            bias = bias_tile(j)
            return tuple(acc + _dot(vt, jnp.exp(s + bias).astype(BF16)) for s, acc in zip(ss, accs))

        accs = _two_slot_pipeline(nk, scratch[0], scores, weighted, (zero, zero))
    else:
        def body(j, states):
            ks, vt = tiles(j)
            bias = bias_tile(j)
            return tuple(_online_step(k, qt, vt, st, bias) for k, qt, st in zip(ks, qts, states))
        init = (jnp.full((1, tq), NEG_BIG, F32), zero)
        accs = [st[1] for st in lax.fori_loop(0, nk, body, (init, init))]

    o1, o2 = [acc[:B_V_DIM] / acc[B_V_DIM:B_V_DIM + 1] for acc in accs]
    lp = lam_ref[...]
    lam = (jnp.exp(jnp.sum(lp[0:1] * lp[1:2], axis=-1, keepdims=True))
           - jnp.exp(jnp.sum(lp[2:3] * lp[3:4], axis=-1, keepdims=True)) + LAMBDA_INIT)
    out = (o1 - lam * o2).T
    out = out * lax.rsqrt(jnp.mean(out * out, axis=-1, keepdims=True) + NORM_EPS) * gsub_ref[...]
    o_ref[...] = (out * (1.0 - LAMBDA_INIT)).astype(BF16)


def _t5_bucket(rel):
    nb = REL_BUCKETS // 2
    ret = (rel > 0).astype(I32) * nb
    n = jnp.abs(rel)
    max_exact = nb // 2
    nf = jnp.maximum(n, 1).astype(F32)
    large = max_exact + (jnp.log(nf / max_exact) / math.log(REL_MAX_DIST / max_exact)
                         * (nb - max_exact)).astype(I32)
    large = jnp.minimum(large, nb - 1)
    return ret + jnp.where(n < max_exact, n, large)


def _bias_tiles(rel_bias, tq, tk):
    dstep = math.gcd(tq, tk)
    reach = (tq + tk + REL_MAX_DIST) // dstep
    band = [d for d in range(-reach * dstep, reach * dstep + 1, dstep)
            if d + tk - 1 > -REL_MAX_DIST and d - (tq - 1) < REL_MAX_DIST]
    far = tq + tk + REL_MAX_DIST
    ds = jnp.asarray([-far] + band + [far], I32)
    rel = (ds[:, None, None] + jnp.arange(tk, dtype=I32)[None, :, None]
           - jnp.arange(tq, dtype=I32)[None, None, :])
    bucket = _t5_bucket(rel)
    table = rel_bias.astype(F32)
    table = table - jnp.max(table, axis=0, keepdims=True)
    tiles = jnp.zeros((table.shape[1],) + bucket.shape, F32)
    for b in range(REL_BUCKETS):
        tiles = jnp.where(bucket[None] == b, table[b][:, None, None, None], tiles)
    return tiles, band[0], dstep


def _attn_b(kmax, qb, kb, vbt, tiles, lam_params, gsub, *, bsz, n, d0, dstep, fixed_shift):
    ntok = qb.shape[0]
    tk, tq = tiles.shape[2], tiles.shape[3]
    nq = n // tq
    return pl.pallas_call(
        functools.partial(_attn_b_kernel, tk=tk, d0=d0, dstep=dstep, fixed_shift=fixed_shift),
        grid=(bsz, B_HEADS, nq),
        in_specs=[
            pl.BlockSpec((1, LANES), lambda b, h, i: (0, 0)),
            pl.BlockSpec((tq, LANES), lambda b, h, i: (b * nq + i, h)),
            pl.BlockSpec((1, 2, n, LANES), lambda b, h, i: (h, 0, b, 0)),
            pl.BlockSpec((1, vbt.shape[1], n), lambda b, h, i: (h, 0, b)),
            pl.BlockSpec((1,) + tiles.shape[1:], lambda b, h, i: (h, 0, 0, 0)),
            pl.BlockSpec(lam_params.shape, lambda b, h, i: (0, 0)),
            pl.BlockSpec((1, LANES), lambda b, h, i: (0, 0)),
        ],
        out_specs=pl.BlockSpec((tq, LANES), lambda b, h, i: (b * nq + i, h)),
        out_shape=jax.ShapeDtypeStruct((ntok, B_V_COLS), BF16),
        scratch_shapes=_score_scratch(fixed_shift, 2, tq, tk, n),
        compiler_params=_cparams(("arbitrary", "arbitrary", "arbitrary")),
        name="attn_b_shift" if fixed_shift else "attn_b_online",
    )(kmax, qb, kb, vbt, tiles, lam_params, gsub)


def _outproj_kernel(x_ref, a_ref, b_ref, wo_ref, gffn_ref, wr_hi_ref, wr_lo_ref,
                    x1_ref, h2_ref, afft_ref, *, n_experts):
    wa = wo_ref.shape[0] // 2
    x1 = x_ref[...] + _dot(a_ref[...], wo_ref[:wa]) + _dot(b_ref[...], wo_ref[wa:])
    x1_ref[...] = x1
    h2 = x1 * lax.rsqrt(jnp.mean(x1 * x1, axis=-1, keepdims=True) + NORM_EPS) * gffn_ref[...]
    h2_ref[...] = h2.astype(BF16)
    hi = h2.astype(BF16)
    lo = (h2 - hi.astype(F32)).astype(BF16)
    logits = _dot(hi, wr_hi_ref[...]) + _dot(hi, wr_lo_ref[...]) + _dot(lo, wr_hi_ref[...])
    lane = lax.broadcasted_iota(I32, (1, LANES), 1)
    logits = jnp.where(lane < n_experts, logits, NEG_BIG)
    ex = jnp.exp(logits - jnp.max(logits, axis=-1, keepdims=True))
    aff = ex / jnp.sum(ex, axis=-1, keepdims=True)
    afft_ref[...] = aff.T[:n_experts]


def _outproj(x, a_out, b_out, w_out, gffn, wr_hi, wr_lo, n_experts):
    ntok, d = x.shape
    tm = min(TOKEN_TILE, ntok)
    tok = lambda i: (i, 0)
    full = lambda i: (0, 0)
    return pl.pallas_call(
        functools.partial(_outproj_kernel, n_experts=n_experts),
        grid=(ntok // tm,),
        in_specs=[
            pl.BlockSpec((tm, d), tok),
            pl.BlockSpec((tm, a_out.shape[1]), tok),
            pl.BlockSpec((tm, b_out.shape[1]), tok),
            pl.BlockSpec(w_out.shape, full),
            pl.BlockSpec((1, d), full),
            pl.BlockSpec(wr_hi.shape, full),
            pl.BlockSpec(wr_lo.shape, full),
        ],
        out_specs=[
            pl.BlockSpec((tm, d), tok),
            pl.BlockSpec((tm, d), tok),
            pl.BlockSpec((n_experts, tm), lambda i: (0, i)),
        ],
        out_shape=[
            jax.ShapeDtypeStruct((ntok, d), F32),
            jax.ShapeDtypeStruct((ntok, d), BF16),
            jax.ShapeDtypeStruct((n_experts, ntok), F32),
        ],
        compiler_params=_cparams(("arbitrary",)),
        name="outproj",
    )(x, a_out, b_out, w_out, gffn, wr_hi, wr_lo)


def _topk_kernel(aff_ref, slot_ref, gsel_ref, cstart_ref, *, cap):
    aff = aff_ref[...]
    n_exp, ntok = aff.shape
    bits = lax.bitcast_convert_type(aff, I32)

    def count(mask):
        return jnp.sum(mask.astype(I32), axis=1, keepdims=True)

    def value_bit(i, t):
        cand = t | lax.shift_left(jnp.int32(1), 30 - i)
        return jnp.where(count(bits >= cand) >= cap, cand, t)

    thr = lax.fori_loop(0, 31, value_bit, jnp.zeros((n_exp, 1), I32))
    above = bits > thr
    tie = bits == thr
    need = cap - count(above)
    idx = lax.broadcasted_iota(I32, aff.shape, 1)

    n_bits = int(ntok).bit_length()

    def index_bit(i, bound):
        cand = bound + lax.shift_left(jnp.int32(1), n_bits - 1 - i)
        ok = (cand <= ntok) & (count(tie & (idx < cand)) <= need)
        return jnp.where(ok, cand, bound)

    bound = lax.fori_loop(0, n_bits, index_bit, jnp.zeros((n_exp, 1), I32))
    sel = above | (tie & (idx < bound))
    gsel_ref[...] = jnp.where(sel, aff, 0.0)

    r = lax.broadcasted_iota(I32, (SUB, SUB), 0)
    c = lax.broadcasted_iota(I32, (SUB, SUB), 1)
    before = jnp.where(r < c, 1.0, 0.0).astype(BF16)
    carry = jnp.zeros((n_exp, 1), F32)
    n_sub = ntok // SUB
    for s in range(n_sub):
        sc = sel[:, s * SUB:(s + 1) * SUB]
        scf = jnp.where(sc, 1.0, 0.0)
        pos = (_dot(scf.astype(BF16), before) + carry).astype(I32)
        slot_ref[:, s * SUB:(s + 1) * SUB] = jnp.where(sc, pos, -1)
        cstart_ref[:, s:s + 1] = carry.astype(I32)
        carry = carry + jnp.sum(scf, axis=1, keepdims=True)


def _topk(afft, cap):
    n_exp, ntok = afft.shape
    nsub = ntok // SUB
    return pl.pallas_call(
        functools.partial(_topk_kernel, cap=cap),
        out_shape=[
            jax.ShapeDtypeStruct((n_exp, ntok), I32),
            jax.ShapeDtypeStruct((n_exp, ntok), F32),
            jax.ShapeDtypeStruct((n_exp, nsub), I32),
        ],
        compiler_params=pltpu.CompilerParams(vmem_limit_bytes=VMEM_LIMIT_BYTES),
        name="topk",
    )(afft)


def _gather_kernel(cstart_ref, slot_ref, gsel_ref, h_ref, xe_ref, gs_ref, acc_ref, gacc_ref, *, cap):
    e = pl.program_id(0)
    i = pl.program_id(1)
    n_sub = h_ref.shape[0] // SUB

    @pl.when(i == 0)
    def _():
        acc_ref[...] = jnp.zeros_like(acc_ref)
        gacc_ref[...] = jnp.zeros_like(gacc_ref)

    row = lax.broadcasted_iota(I32, (GATHER_WIN, SUB), 0)
    for s in range(n_sub):
        cs = cstart_ref[e, i * n_sub + s]
        base = pl.multiple_of(lax.div(cs, 16) * 16, 16)
        sl = slot_ref[0, :, s * SUB:(s + 1) * SUB]
        hit = row == (sl - base)
        onehot = jnp.where(hit, 1.0, 0.0).astype(BF16)
        acc_ref[pl.ds(base, GATHER_WIN), :] += _dot(onehot, h_ref[s * SUB:(s + 1) * SUB, :])
        g = gsel_ref[0, :, s * SUB:(s + 1) * SUB]
        gw = jnp.sum(jnp.where(hit, g, 0.0), axis=1, keepdims=True)
        gacc_ref[pl.ds(base, GATHER_WIN), :] += jnp.broadcast_to(gw, (GATHER_WIN, LANES))

    @pl.when(i == pl.num_programs(1) - 1)
    def _():
        xe_ref[0] = acc_ref[:cap].astype(BF16)
        gs_ref[0] = gacc_ref[:cap]


def _gather(cstart, slot3, gsel3, h2, cap):
    n_exp = slot3.shape[0]
    ntok, d = h2.shape
    tb = min(GATHER_TB, ntok)
    return pl.pallas_call(
        functools.partial(_gather_kernel, cap=cap),
        grid_spec=pltpu.PrefetchScalarGridSpec(
            num_scalar_prefetch=1,
            grid=(n_exp, ntok // tb),
            in_specs=[
                pl.BlockSpec((1, 1, tb), lambda e, i, cs: (e, 0, i)),
                pl.BlockSpec((1, 1, tb), lambda e, i, cs: (e, 0, i)),
                pl.BlockSpec((tb, d), lambda e, i, cs: (i, 0)),
            ],
            out_specs=[
                pl.BlockSpec((1, cap, d), lambda e, i, cs: (e, 0, 0)),
                pl.BlockSpec((1, cap, LANES), lambda e, i, cs: (e, 0, 0)),
            ],
            scratch_shapes=[
                pltpu.VMEM((cap + GATHER_WIN, d), F32),
                pltpu.VMEM((cap + GATHER_WIN, LANES), F32),
            ],
        ),
        out_shape=[
            jax.ShapeDtypeStruct((n_exp, cap, d), BF16),
            jax.ShapeDtypeStruct((n_exp, cap, LANES), F32),
        ],
        compiler_params=_cparams(("arbitrary", "arbitrary")),
        name="gather",
    )(cstart, slot3, gsel3, h2)


def _ffn_kernel(xe_ref, gs_ref, wg_ref, wu_ref, wd_ref, ye_ref, acc_ref):
    f = pl.program_id(1)
    cap = xe_ref.shape[1]
    rows = min(FFN_ROWS, cap)

    @pl.when(f == 0)
    def _():
        acc_ref[...] = jnp.zeros_like(acc_ref)

    wg = wg_ref[0].astype(BF16)
    wu = wu_ref[0].astype(BF16)
    wd = wd_ref[0].astype(BF16)
    for c in range(cap // rows):
        x = xe_ref[0, c * rows:(c + 1) * rows, :]
        g = _dot(x, wg)
        u = _dot(x, wu)
        hid = (g * jax.nn.sigmoid(g) * u).astype(BF16)
        acc_ref[c * rows:(c + 1) * rows, :] += _dot(hid, wd)

    @pl.when(f == pl.num_programs(1) - 1)
    def _():
        ye_ref[0] = (acc_ref[...] * gs_ref[0][:, :1]).astype(BF16)


def _ffn(xe, gs, w_gate, w_up, w_down):
    n_exp, cap, d = xe.shape
    ff = w_gate.shape[-1]
    tf = min(FFN_TF, ff)
    return pl.pallas_call(
        _ffn_kernel,
        grid=(n_exp, ff // tf),
        in_specs=[
            pl.BlockSpec((1, cap, d), lambda e, f: (e, 0, 0)),
            pl.BlockSpec((1, cap, LANES), lambda e, f: (e, 0, 0)),
            pl.BlockSpec((1, d, tf), lambda e, f: (e, 0, f)),
            pl.BlockSpec((1, d, tf), lambda e, f: (e, 0, f)),
            pl.BlockSpec((1, tf, d), lambda e, f: (e, f, 0)),
        ],
        out_specs=pl.BlockSpec((1, cap, d), lambda e, f: (e, 0, 0)),
        out_shape=jax.ShapeDtypeStruct((n_exp, cap, d), BF16),
        scratch_shapes=[pltpu.VMEM((cap, d), F32)],
        compiler_params=_cparams(("arbitrary", "arbitrary")),
        name="ffn",
    )(xe, gs, w_gate, w_up, w_down)


def _window_start(first_slot, cap, rows):
    return jnp.minimum(lax.div(first_slot, 16), (cap - rows) // 16) * 16


def _combine_kernel(cstart_ref, x1_ref, slot_ref, ye_ref, gfin_ref, y_ref, acc_ref, *, cap):
    i = pl.program_id(0)
    e = pl.program_id(1)
    n_sub = x1_ref.shape[0] // SUB
    rows = ye_ref.shape[1]
    start = _window_start(cstart_ref[e, i * n_sub], cap, rows)

    @pl.when(e == 0)
    def _():
        acc_ref[...] = x1_ref[...]

    row = lax.broadcasted_iota(I32, (COMBINE_WIN, SUB), 0)
    dims = (((0,), (0,)), ((), ()))
    for s in range(n_sub):
        cs16 = lax.div(cstart_ref[e, i * n_sub + s], 16) * 16
        w0 = pl.multiple_of(jnp.clip(cs16 - start, 0, rows - COMBINE_WIN), 16)
        sl = slot_ref[0, :, s * SUB:(s + 1) * SUB]
        onehot = jnp.where(row == (sl - (start + w0)), 1.0, 0.0).astype(BF16)
        win = ye_ref[0, pl.ds(w0, COMBINE_WIN), :]
        acc_ref[s * SUB:(s + 1) * SUB, :] += lax.dot_general(onehot, win, dims,
                                                             preferred_element_type=F32)

    @pl.when(e == pl.num_programs(1) - 1)
    def _():
        out = acc_ref[...]
        y_ref[...] = out * lax.rsqrt(jnp.mean(out * out, axis=-1, keepdims=True) + NORM_EPS) * gfin_ref[...]


def _combine(cstart, x1, slot3, ye, gfin):
    n_exp, cap, d = ye.shape
    ntok = x1.shape[0]
    tb = min(COMBINE_TB, ntok, cap // 2)
    rows = tb + 16
    n_sub = tb // SUB
    assert COMBINE_WIN <= rows <= cap and COMBINE_WIN >= SUB + 16
    return pl.pallas_call(
        functools.partial(_combine_kernel, cap=cap),
        grid_spec=pltpu.PrefetchScalarGridSpec(
            num_scalar_prefetch=1,
            grid=(ntok // tb, n_exp),
            in_specs=[
                pl.BlockSpec((tb, d), lambda i, e, cs: (i, 0)),
                pl.BlockSpec((1, 1, tb), lambda i, e, cs: (e, 0, i)),
                pl.BlockSpec((pl.Element(1), pl.Element(rows), pl.Element(d)),
                             lambda i, e, cs: (e, _window_start(cs[e, i * n_sub], cap, rows), 0)),
                pl.BlockSpec((1, d), lambda i, e, cs: (0, 0)),
            ],
            out_specs=pl.BlockSpec((tb, d), lambda i, e, cs: (i, 0)),
            scratch_shapes=[pltpu.VMEM((tb, d), F32)],
        ),
        out_shape=jax.ShapeDtypeStruct((ntok, d), F32),
        compiler_params=_cparams(("arbitrary", "arbitrary")),
        name="combine",
    )(cstart, x1, slot3, ye, gfin)


def _rope_tables(n):
    rows = n // GRID_W
    row = jnp.broadcast_to(jnp.arange(rows)[:, None], (rows, GRID_W)).reshape(-1).astype(F32)
    col = jnp.broadcast_to(jnp.arange(GRID_W)[None, :], (rows, GRID_W)).reshape(-1).astype(F32)
    half = HEAD_DIM // 2
    inv = ROPE_THETA ** (-jnp.arange(0, half, 2, dtype=F32) / half)
    ang_r = row[:, None] * inv[None, :]
    ang_c = col[:, None] * inv[None, :]
    cos = jnp.concatenate([jnp.cos(ang_r), jnp.cos(ang_r), jnp.cos(ang_c), jnp.cos(ang_c)], axis=-1)
    sin = jnp.concatenate([-jnp.sin(ang_r), jnp.sin(ang_r), -jnp.sin(ang_c), jnp.sin(ang_c)], axis=-1)
    return jnp.tile(cos, (1, 2)), jnp.tile(sin, (1, 2))


def _encoder(x, p):
    bsz, n, d = x.shape
    ntok = bsz * n
    n_exp = p["n_exp"]
    cap = CAPACITY_FACTOR * ntok // n_exp
    xt = x.reshape(ntok, d)
    cos, sin = _rope_tables(n)
    qa, ka, vat, qb, kb, vbt, stats = _proj(xt, p["gmix"], p["w_in"], p["gq"], p["gk"], cos, sin,
                                            p["seg"], n)

    qa_n, ka_n, qb_n, kb_n = jnp.sqrt(jnp.max(stats[:, :4, 0], axis=0)) * NORM_SLACK
    lanes = lambda v: jnp.broadcast_to(v, (1, LANES)).astype(F32)
    a_args = (lanes(ka_n), qa, ka, vat)
    a_out = lax.cond(qa_n * ka_n <= SAFE_RANGE,
                     functools.partial(_attn_a, bsz=bsz, n=n, fixed_shift=True),
                     functools.partial(_attn_a, bsz=bsz, n=n, fixed_shift=False), *a_args)
    tiles, d0, dstep = p["bias_tiles"](min(ATTN_TQ, n), min(ATTN_TK, n))
    b_args = (lanes(kb_n), qb, kb, vbt, tiles, p["lam"], p["gsub"])
    b_kw = dict(bsz=bsz, n=n, d0=d0, dstep=dstep)
    b_out = lax.cond(2.0 * qb_n * kb_n + p["bias_spread"] <= 2.0 * SAFE_RANGE,
                     functools.partial(_attn_b, fixed_shift=True, **b_kw),
                     functools.partial(_attn_b, fixed_shift=False, **b_kw), *b_args)

    x1, h2, afft = _outproj(xt, a_out, b_out, p["w_out"], p["gffn"], p["wr_hi"], p["wr_lo"], n_exp)
    slot, gsel, cstart = _topk(afft, cap)
    slot3 = slot.reshape(n_exp, 1, ntok)
    gsel3 = gsel.reshape(n_exp, 1, ntok)
    xe, gs = _gather(cstart, slot3, gsel3, h2, cap)
    ye = _ffn(xe, gs, p["w_gate"], p["w_up"], p["w_down"])
    y = _combine(cstart, x1, slot3, ye, p["gfin"])
    return y.reshape(bsz, n, d)


def kernel(x_prompt, x_sample, norm_mix, w_in, a_q_norm, a_k_norm, b_lambda, b_subln, w_out, rel_bias,
           norm_ffn, w_router, w_gate, w_up, w_down, norm_final):
    d = x_prompt.shape[-1]
    n_exp = w_router.shape[-1]
    wr = jnp.zeros((d, LANES), F32).at[:, :n_exp].set(w_router[0].astype(F32))
    wr_hi = wr.astype(BF16)
    lane_head = np.arange(LANES) // HEAD_DIM
    tile_cache = {}

    def bias_tiles(tq, tk):
        if (tq, tk) not in tile_cache:
            tile_cache[(tq, tk)] = _bias_tiles(rel_bias, tq, tk)
        return tile_cache[(tq, tk)]

    p = dict(
        n_exp=n_exp,
        gmix=norm_mix[0].astype(F32).reshape(1, d),
        w_in=w_in[0].astype(BF16),
        gq=jnp.tile(a_q_norm[0].astype(F32), 2).reshape(1, LANES),
        gk=jnp.tile(a_k_norm[0].astype(F32), 2).reshape(1, LANES),
        seg=jnp.asarray(lane_head[:, None] == lane_head[None, :], BF16),
        bias_tiles=bias_tiles,
        bias_spread=(jnp.max(rel_bias) - jnp.min(rel_bias)).astype(F32),
        lam=b_lambda[0].astype(F32),
        gsub=b_subln[0].astype(F32).reshape(1, LANES),
        w_out=w_out[0].astype(BF16),
        gffn=norm_ffn[0].astype(F32).reshape(1, d),
        wr_hi=wr_hi,
        wr_lo=(wr - wr_hi.astype(F32)).astype(BF16),
        w_gate=w_gate[0], w_up=w_up[0], w_down=w_down[0],
        gfin=norm_final.astype(F32).reshape(1, d),
    )
    return (_encoder(x_prompt, p), _encoder(x_sample, p))
```

```python
import functools
import math

import jax
import jax.numpy as jnp
import numpy as np
from jax import lax
from jax.experimental import pallas as pl
from jax.experimental.pallas import tpu as pltpu

F32 = jnp.float32
BF16 = jnp.bfloat16
I32 = jnp.int32

HEAD_DIM = 64
A_Q_HEADS = 8
A_KV_HEADS = 2
B_HEADS = 4
A_Q_COLS = A_Q_HEADS * HEAD_DIM
A_KV_COLS = A_KV_HEADS * HEAD_DIM
B_QK_COLS = B_HEADS * 2 * HEAD_DIM
B_V_DIM = 2 * HEAD_DIM
B_V_COLS = B_HEADS * B_V_DIM
GRID_W = 64
ROPE_THETA = 10000.0
REL_BUCKETS = 32
REL_MAX_DIST = 128
CAPACITY_FACTOR = 2
NORM_EPS = 1e-6
LAMBDA_INIT = 0.8 - 0.6 * math.exp(-0.3 * 0)
QK_SCALE = HEAD_DIM ** -0.5

LANES = 128
BF16_ROWS = 16
VMEM_LIMIT_BYTES = 56 * 1024 * 1024

TOKEN_TILE = 512
PROJ_ROW_CHUNKS = 1
OUTPROJ_ROW_CHUNKS = 2
ATTN_TQ = 512
ATTN_TK = 512
TILES_PER_BODY = 4
SUB = 128
GATHER_WIN = SUB + 16
GATHER_WIN_SPARSE = 64
GATHER_TB = 2048
COMBINE_TB = 512
COMBINE_WIN = 256
COMBINE_ROWS_SPARSE = 144
FFN_TF = 256
FFN_ROWS = 512
NEG_BIG = -1e30

SAFE_RANGE = 40.0
NORM_SLACK = 1.0 + 2.0 ** -7


def _dot(a, b):
    return jnp.dot(a, b, preferred_element_type=F32)


def _cparams(sem):
    return pltpu.CompilerParams(dimension_semantics=sem, vmem_limit_bytes=VMEM_LIMIT_BYTES)


def _split_dot(v, w_bf16):
    hi = v.astype(BF16)
    lo = (v - hi.astype(F32)).astype(BF16)
    return _dot(hi, w_bf16) + _dot(lo, w_bf16)


def _proj_kernel(x_ref, gmix_ref, w_ref, gq_ref, gk_ref, cos_ref, sin_ref, seg_ref,
                 qa_ref, ka_ref, vat_ref, qb_ref, kb_ref, vbt_ref, st_ref):
    seg = seg_ref[...]
    lane = lax.broadcasted_iota(I32, (1, LANES), 1)
    first_half = (lane % 32) < 16
    low_head = lane < HEAD_DIM
    ones_lane64 = jnp.where(lane == HEAD_DIM, 1.0, 0.0)

    def max_head_sumsq(vals):
        sq = vals.astype(F32)
        return jnp.max(_dot((sq * sq).astype(BF16), seg), axis=0, keepdims=True)

    def key_rows(k_pair, odd):
        kh = pltpu.roll(k_pair, HEAD_DIM, 1) if odd else k_pair
        return jnp.where(low_head, kh, ones_lane64).astype(BF16)

    def chunk(rows):
        x = x_ref[rows, :]
        h = (x * lax.rsqrt(jnp.mean(x * x, axis=-1, keepdims=True) + NORM_EPS) * gmix_ref[...]).astype(BF16)
        cos = cos_ref[rows, :]
        sin = sin_ref[rows, :]
        ones_row = jnp.where(lax.broadcasted_iota(I32, (BF16_ROWS, x.shape[0]), 0) == 0, 1.0, 0.0)

        def norm_rope(y, gain):
            ss = _split_dot(y * y, seg)
            y = y * lax.rsqrt(ss * (1.0 / HEAD_DIM) + NORM_EPS) * gain
            rot = jnp.where(first_half, pltpu.roll(y, LANES - 16, 1), pltpu.roll(y, 16, 1))
            return y * cos + rot * sin

        def value_rows_t(vt_rows):
            return jnp.concatenate([vt_rows, ones_row], axis=0).astype(BF16)

        c0 = 0
        aq = _dot(h, w_ref[:, c0:c0 + A_Q_COLS])
        qa_stat = jnp.zeros((1, LANES), F32)
        for c in range(A_Q_COLS // LANES):
            y = (norm_rope(aq[:, c * LANES:(c + 1) * LANES], gq_ref[...]) * QK_SCALE).astype(BF16)
            qa_ref[rows, c * LANES:(c + 1) * LANES] = y
            qa_stat = jnp.maximum(qa_stat, max_head_sumsq(y))
        c0 += A_Q_COLS

        akv = _dot(h, w_ref[:, c0:c0 + 2 * A_KV_COLS])
        k = norm_rope(akv[:, :LANES], gk_ref[...]).astype(BF16)
        ka_stat = max_head_sumsq(k)
        k = k.astype(F32)
        vt = akv[:, LANES:].T
        for g in range(A_KV_HEADS):
            ka_ref[g, rows, :] = key_rows(k, g)
            vat_ref[g, :, rows] = value_rows_t(vt[g * HEAD_DIM:(g + 1) * HEAD_DIM])
        c0 += 2 * A_KV_COLS

        qb = (_dot(h, w_ref[:, c0:c0 + B_QK_COLS]) * QK_SCALE).astype(BF16)
        qb_ref[rows, :] = qb
        c0 += B_QK_COLS
        bk = _dot(h, w_ref[:, c0:c0 + B_QK_COLS]).astype(BF16)
        c0 += B_QK_COLS
        bv = _dot(h, w_ref[:, c0:c0 + B_V_COLS])
        qb_stat = jnp.zeros((1, LANES), F32)
        kb_stat = jnp.zeros((1, LANES), F32)
        for hd in range(B_HEADS):
            blk = slice(hd * LANES, (hd + 1) * LANES)
            qb_stat = jnp.maximum(qb_stat, max_head_sumsq(qb[:, blk]))
            kb_stat = jnp.maximum(kb_stat, max_head_sumsq(bk[:, blk]))
            k_pair = bk[:, blk].astype(F32)
            kb_ref[hd, 0, rows, :] = key_rows(k_pair, 0)
            kb_ref[hd, 1, rows, :] = key_rows(k_pair, 1)
            vbt_ref[hd, :, rows] = value_rows_t(bv[:, blk].T)
        return qa_stat, ka_stat, qb_stat, kb_stat

    per_chunk = [chunk(rows) for rows in _row_chunks(x_ref.shape[0], PROJ_ROW_CHUNKS)]
    stats = [functools.reduce(jnp.maximum, s) for s in zip(*per_chunk)]
    stats = [jnp.broadcast_to(jnp.max(s, axis=1, keepdims=True), (1, LANES)) for s in stats]
    st_ref[0] = jnp.concatenate(stats + [jnp.zeros((8 - len(stats), LANES), F32)], axis=0)


def _proj(x, gmix, w_in, gq, gk, cos, sin, seg, n):
    ntok, d = x.shape
    tm = min(TOKEN_TILE, n)
    per_seq = n // tm
    ntiles = ntok // tm
    va_rows = HEAD_DIM + BF16_ROWS
    vb_rows = B_V_DIM + BF16_ROWS
    tok = lambda i: (i, 0)
    full = lambda i: (0, 0)
    return pl.pallas_call(
        _proj_kernel,
        grid=(ntiles,),
        in_specs=[
            pl.BlockSpec((tm, d), tok),
            pl.BlockSpec((1, d), full),
            pl.BlockSpec(w_in.shape, full),
            pl.BlockSpec((1, LANES), full),
            pl.BlockSpec((1, LANES), full),
            pl.BlockSpec((tm, LANES), lambda i: (i % per_seq, 0)),
            pl.BlockSpec((tm, LANES), lambda i: (i % per_seq, 0)),
            pl.BlockSpec((LANES, LANES), full),
        ],
        out_specs=[
            pl.BlockSpec((tm, A_Q_COLS), tok),
            pl.BlockSpec((A_KV_HEADS, tm, LANES), lambda i: (0, i, 0)),
            pl.BlockSpec((A_KV_HEADS, va_rows, tm), lambda i: (0, 0, i)),
            pl.BlockSpec((tm, B_QK_COLS), tok),
            pl.BlockSpec((B_HEADS, 2, tm, LANES), lambda i: (0, 0, i, 0)),
            pl.BlockSpec((B_HEADS, vb_rows, tm), lambda i: (0, 0, i)),
            pl.BlockSpec((1, 8, LANES), lambda i: (i, 0, 0)),
        ],
        out_shape=[
            jax.ShapeDtypeStruct((ntok, A_Q_COLS), BF16),
            jax.ShapeDtypeStruct((A_KV_HEADS, ntok, LANES), BF16),
            jax.ShapeDtypeStruct((A_KV_HEADS, va_rows, ntok), BF16),
            jax.ShapeDtypeStruct((ntok, B_QK_COLS), BF16),
            jax.ShapeDtypeStruct((B_HEADS, 2, ntok, LANES), BF16),
            jax.ShapeDtypeStruct((B_HEADS, vb_rows, ntok), BF16),
            jax.ShapeDtypeStruct((ntiles, 8, LANES), F32),
        ],
        compiler_params=_cparams(("arbitrary",)),
        name="proj",
    )(x, gmix, w_in, gq, gk, cos, sin, seg)


def _query_operands(q_pair, kmax, fixed_shift):
    qt = q_pair.astype(F32).T
    tq = qt.shape[1]
    first_row = lax.broadcasted_iota(I32, (HEAD_DIM, tq), 0) == 0
    out = []
    for half in (qt[:HEAD_DIM], qt[HEAD_DIM:]):
        tail = jnp.zeros((HEAD_DIM, tq), F32)
        if fixed_shift:
            shift = jnp.sqrt(jnp.sum(half * half, axis=0, keepdims=True)) * kmax
            tail = jnp.where(first_row, -shift, tail)
        out.append(jnp.concatenate([half, tail], axis=0).astype(BF16))
    return out


def _online_step(k, qt, vt, state, bias=None):
    m, acc = state
    s = _dot(k, qt)
    if bias is not None:
        s = s + bias
    m_new = jnp.maximum(m, jnp.max(s, axis=0, keepdims=True))
    acc = jnp.exp(m - m_new) * acc + _dot(vt, jnp.exp(s - m_new).astype(BF16))
    return m_new, acc


def _two_slot_pipeline(nk, s_ref, score_fn, weigh_fn, accs):
    n_maps = s_ref.shape[1]

    def put(slot, j):
        for h, s in enumerate(score_fn(j)):
            s_ref[slot, h] = s

    def take(slot, j, accs):
        return weigh_fn(j, tuple(s_ref[slot, h] for h in range(n_maps)), accs)

    def run(first, count, accs):
        for t in range(count):
            put((t + 1) % 2, first + t + 1)
            accs = take(t % 2, first + t, accs)
        return accs

    put(0, 0)
    accs = lax.fori_loop(0, nk // TILES_PER_BODY - 1,
                         lambda i, accs: run(TILES_PER_BODY * i, TILES_PER_BODY, accs), accs)
    accs = run(nk - TILES_PER_BODY, TILES_PER_BODY - 1, accs)
    return take((TILES_PER_BODY - 1) % 2, nk - 1, accs)


def _score_scratch(fixed_shift, n_maps, tq, tk, n):
    if not fixed_shift:
        return []
    assert (n // tk) % TILES_PER_BODY == 0 and TILES_PER_BODY % 2 == 0
    return [pltpu.VMEM((2, n_maps, tk, tq), F32)]


def _attn_a_kernel(kmax_ref, q_ref, k_ref, vt_ref, o_ref, *scratch, tk, fixed_shift):
    tq = q_ref.shape[0]
    nk = k_ref.shape[1] // tk
    n_pairs = q_ref.shape[1] // LANES
    kmax = kmax_ref[:, :1]
    qts = []
    for pr in range(n_pairs):
        qts += _query_operands(q_ref[:, pr * LANES:(pr + 1) * LANES], kmax, fixed_shift)

    def tiles(j):
        off = pl.multiple_of(j * tk, tk)
        return k_ref[0, pl.ds(off, tk), :], vt_ref[0, :, pl.ds(off, tk)]

    zero = jnp.zeros((vt_ref.shape[1], tq), F32)
    if fixed_shift:
        def scores(j):
            k = tiles(j)[0]
            return tuple(_dot(k, qt) for qt in qts)

        def weighted(j, ss, accs):
            vt = tiles(j)[1]
            return tuple(acc + _dot(vt, jnp.exp(s).astype(BF16)) for s, acc in zip(ss, accs))

        accs = _two_slot_pipeline(nk, scratch[0], scores, weighted, tuple(zero for _ in qts))
    else:
        def body(j, states):
            k, vt = tiles(j)
            return tuple(_online_step(k, qt, vt, st) for qt, st in zip(qts, states))
        init = (jnp.full((1, tq), NEG_BIG, F32), zero)
        accs = [st[1] for st in lax.fori_loop(0, nk, body, tuple(init for _ in qts))]

    for pr in range(n_pairs):
        halves = [acc[:HEAD_DIM] / acc[HEAD_DIM:HEAD_DIM + 1] for acc in accs[2 * pr:2 * pr + 2]]
        o_ref[:, pr * LANES:(pr + 1) * LANES] = jnp.concatenate(halves, axis=0).T.astype(BF16)


def _attn_a(kmax, qa, ka, vat, *, bsz, n, fixed_shift):
    ntok = qa.shape[0]
    tq = min(ATTN_TQ, n)
    tk = min(ATTN_TK, n)
    nq = n // tq
    gcols = A_Q_COLS // A_KV_HEADS
    return pl.pallas_call(
        functools.partial(_attn_a_kernel, tk=tk, fixed_shift=fixed_shift),
        grid=(bsz, A_KV_HEADS, nq),
        in_specs=[
            pl.BlockSpec((1, LANES), lambda b, g, i: (0, 0)),
            pl.BlockSpec((tq, gcols), lambda b, g, i: (b * nq + i, g)),
            pl.BlockSpec((1, n, LANES), lambda b, g, i: (g, b, 0)),
            pl.BlockSpec((1, vat.shape[1], n), lambda b, g, i: (g, 0, b)),
        ],
        out_specs=pl.BlockSpec((tq, gcols), lambda b, g, i: (b * nq + i, g)),
        out_shape=jax.ShapeDtypeStruct((ntok, A_Q_COLS), BF16),
        scratch_shapes=_score_scratch(fixed_shift, A_Q_HEADS // A_KV_HEADS, tq, tk, n),
        compiler_params=_cparams(("arbitrary", "arbitrary", "arbitrary")),
        name="attn_a_shift" if fixed_shift else "attn_a_online",
    )(kmax, qa, ka, vat)


def _attn_b_kernel(kmax_ref, q_ref, k_ref, vt_ref, bias_ref, lam_ref, gsub_ref, o_ref, *scratch,
                   tk, d0, dstep, fixed_shift):
    tq = q_ref.shape[0]
    nk = k_ref.shape[2] // tk
    n_tiles = bias_ref.shape[1]
    qi = pl.program_id(2)
    qts = _query_operands(q_ref[...], kmax_ref[:, :1], fixed_shift)

    def tiles(j):
        off = pl.multiple_of(j * tk, tk)
        return (k_ref[0, 0, pl.ds(off, tk), :], k_ref[0, 1, pl.ds(off, tk), :]), vt_ref[0, :, pl.ds(off, tk)]

    def bias_tile(j):
        u = jnp.clip(j * tk - qi * tq - d0 + dstep, 0, (n_tiles - 1) * dstep)
        return bias_ref[0, lax.div(u, dstep)]

    zero = jnp.zeros((vt_ref.shape[1], tq), F32)
    if fixed_shift:
        def scores(j):
            return tuple(_dot(k, qt) for k, qt in zip(tiles(j)[0], qts))

        def weighted(j, ss, accs):
            vt = tiles(j)[1]
            bias = bias_tile(j)
            return tuple(acc + _dot(vt, jnp.exp(s + bias).astype(BF16)) for s, acc in zip(ss, accs))

        accs = _two_slot_pipeline(nk, scratch[0], scores, weighted, (zero, zero))
    else:
        def body(j, states):
            ks, vt = tiles(j)
            bias = bias_tile(j)
            return tuple(_online_step(k, qt, vt, st, bias) for k, qt, st in zip(ks, qts, states))
        init = (jnp.full((1, tq), NEG_BIG, F32), zero)
        accs = [st[1] for st in lax.fori_loop(0, nk, body, (init, init))]

    o1, o2 = [acc[:B_V_DIM] / acc[B_V_DIM:B_V_DIM + 1] for acc in accs]
    lp = lam_ref[...]
    lam = (jnp.exp(jnp.sum(lp[0:1] * lp[1:2], axis=-1, keepdims=True))
           - jnp.exp(jnp.sum(lp[2:3] * lp[3:4], axis=-1, keepdims=True)) + LAMBDA_INIT)
    out = (o1 - lam * o2).T
    out = out * lax.rsqrt(jnp.mean(out * out, axis=-1, keepdims=True) + NORM_EPS) * gsub_ref[...]
    o_ref[...] = (out * (1.0 - LAMBDA_INIT)).astype(BF16)


def _t5_bucket(rel):
    nb = REL_BUCKETS // 2
    ret = (rel > 0).astype(I32) * nb
    n = jnp.abs(rel)
    max_exact = nb // 2
    nf = jnp.maximum(n, 1).astype(F32)
    large = max_exact + (jnp.log(nf / max_exact) / math.log(REL_MAX_DIST / max_exact)
                         * (nb - max_exact)).astype(I32)
    large = jnp.minimum(large, nb - 1)
    return ret + jnp.where(n < max_exact, n, large)


def _bias_tiles(rel_bias, tq, tk):
    dstep = math.gcd(tq, tk)
    reach = (tq + tk + REL_MAX_DIST) // dstep
    band = [d for d in range(-reach * dstep, reach * dstep + 1, dstep)
            if d + tk - 1 > -REL_MAX_DIST and d - (tq - 1) < REL_MAX_DIST]
    far = tq + tk + REL_MAX_DIST
    ds = jnp.asarray([-far] + band + [far], I32)
    span = tq + tk
    rel = ds[:, None] + jnp.arange(span, dtype=I32)[None, :] - (tq - 1)
    bucket = _t5_bucket(rel)
    table = rel_bias.astype(F32)
    table = table - jnp.max(table, axis=0, keepdims=True)
    diag = jnp.zeros((table.shape[1],) + bucket.shape, F32)
    for b in range(REL_BUCKETS):
        diag = jnp.where(bucket[None] == b, table[b][:, None, None], diag)
    skew = jnp.tile(diag, (1, 1, tk + 1))[..., :tk * (span + 1)]
    skew = skew.reshape(diag.shape[:2] + (tk, span + 1))[..., :tq]
    return skew[..., ::-1], band[0], dstep


def _attn_b(kmax, qb, kb, vbt, tiles, lam_params, gsub, *, bsz, n, d0, dstep, fixed_shift):
    ntok = qb.shape[0]
    tk, tq = tiles.shape[2], tiles.shape[3]
    nq = n // tq
    return pl.pallas_call(
        functools.partial(_attn_b_kernel, tk=tk, d0=d0, dstep=dstep, fixed_shift=fixed_shift),
        grid=(bsz, B_HEADS, nq),
        in_specs=[
            pl.BlockSpec((1, LANES), lambda b, h, i: (0, 0)),
            pl.BlockSpec((tq, LANES), lambda b, h, i: (b * nq + i, h)),
            pl.BlockSpec((1, 2, n, LANES), lambda b, h, i: (h, 0, b, 0)),
            pl.BlockSpec((1, vbt.shape[1], n), lambda b, h, i: (h, 0, b)),
            pl.BlockSpec((1,) + tiles.shape[1:], lambda b, h, i: (h, 0, 0, 0)),
            pl.BlockSpec(lam_params.shape, lambda b, h, i: (0, 0)),
            pl.BlockSpec((1, LANES), lambda b, h, i: (0, 0)),
        ],
        out_specs=pl.BlockSpec((tq, LANES), lambda b, h, i: (b * nq + i, h)),
        out_shape=jax.ShapeDtypeStruct((ntok, B_V_COLS), BF16),
        scratch_shapes=_score_scratch(fixed_shift, 2, tq, tk, n),
        compiler_params=_cparams(("arbitrary", "arbitrary", "arbitrary")),
        name="attn_b_shift" if fixed_shift else "attn_b_online",
    )(kmax, qb, kb, vbt, tiles, lam_params, gsub)


def _outproj_kernel(x_ref, a_ref, b_ref, wo_ref, gffn_ref, wr_ref,
                    x1_ref, h2_ref, afft_ref, *, n_experts):
    wa = wo_ref.shape[0] // 2
    lane = lax.broadcasted_iota(I32, (1, LANES), 1)
    for rows in _row_chunks(x_ref.shape[0], OUTPROJ_ROW_CHUNKS):
        x1 = x_ref[rows, :] + _dot(a_ref[rows, :], wo_ref[:wa]) + _dot(b_ref[rows, :], wo_ref[wa:])
        x1_ref[rows, :] = x1
        h2 = x1 * lax.rsqrt(jnp.mean(x1 * x1, axis=-1, keepdims=True) + NORM_EPS) * gffn_ref[...]
        hi = h2.astype(BF16)
        h2_ref[rows, :] = hi
        lo = (h2 - hi.astype(F32)).astype(BF16)
        both = _dot(hi, wr_ref[...])
        logits = both[:, :LANES] + both[:, LANES:] + _dot(lo, wr_ref[:, :LANES])
        logits = jnp.where(lane < n_experts, logits, NEG_BIG)
        ex = jnp.exp(logits - jnp.max(logits, axis=-1, keepdims=True))
        aff = ex / jnp.sum(ex, axis=-1, keepdims=True)
        afft_ref[:, rows] = aff.T[:n_experts]


def _row_chunks(n_rows, n_chunks):
    chunk = max(n_rows // n_chunks, LANES)
    return [slice(r, r + chunk) for r in range(0, n_rows, chunk)]


def _outproj(x, a_out, b_out, w_out, gffn, wr, n_experts):
    ntok, d = x.shape
    tm = min(TOKEN_TILE, ntok)
    tok = lambda i: (i, 0)
    full = lambda i: (0, 0)
    return pl.pallas_call(
        functools.partial(_outproj_kernel, n_experts=n_experts),
        grid=(ntok // tm,),
        in_specs=[
            pl.BlockSpec((tm, d), tok),
            pl.BlockSpec((tm, a_out.shape[1]), tok),
            pl.BlockSpec((tm, b_out.shape[1]), tok),
            pl.BlockSpec(w_out.shape, full),
            pl.BlockSpec((1, d), full),
            pl.BlockSpec(wr.shape, full),
        ],
        out_specs=[
            pl.BlockSpec((tm, d), tok),
            pl.BlockSpec((tm, d), tok),
            pl.BlockSpec((n_experts, tm), lambda i: (0, i)),
        ],
        out_shape=[
            jax.ShapeDtypeStruct((ntok, d), F32),
            jax.ShapeDtypeStruct((ntok, d), BF16),
            jax.ShapeDtypeStruct((n_experts, ntok), F32),
        ],
        compiler_params=_cparams(("arbitrary",)),
        name="outproj",
    )(x, a_out, b_out, w_out, gffn, wr)


def _topk_kernel(aff_ref, slot_ref, gsel_ref, cstart_ref, *, cap):
    aff = aff_ref[...]
    n_exp, ntok = aff.shape
    bits = lax.bitcast_convert_type(aff, I32)

    def count(mask):
        return jnp.sum(mask.astype(I32), axis=1, keepdims=True)

    def value_bit(i, t):
        cand = t | lax.shift_left(jnp.int32(1), 30 - i)
        return jnp.where(count(bits >= cand) >= cap, cand, t)

    thr = lax.fori_loop(0, 31, value_bit, jnp.zeros((n_exp, 1), I32))
    above = bits > thr
    tie = bits == thr
    need = cap - count(above)
    idx = lax.broadcasted_iota(I32, aff.shape, 1)

    n_bits = int(ntok).bit_length()

    def index_bit(i, bound):
        cand = bound + lax.shift_left(jnp.int32(1), n_bits - 1 - i)
        ok = (cand <= ntok) & (count(tie & (idx < cand)) <= need)
        return jnp.where(ok, cand, bound)

    bound = lax.fori_loop(0, n_bits, index_bit, jnp.zeros((n_exp, 1), I32))
    sel = above | (tie & (idx < bound))
    gsel_ref[...] = jnp.where(sel, aff, 0.0)

    r = lax.broadcasted_iota(I32, (SUB, SUB), 0)
    c = lax.broadcasted_iota(I32, (SUB, SUB), 1)
    before = jnp.where(r < c, 1.0, 0.0).astype(BF16)
    carry = jnp.zeros((n_exp, 1), F32)
    n_sub = ntok // SUB
    for s in range(n_sub):
        sc = sel[:, s * SUB:(s + 1) * SUB]
        scf = jnp.where(sc, 1.0, 0.0)
        pos = (_dot(scf.astype(BF16), before) + carry).astype(I32)
        slot_ref[:, s * SUB:(s + 1) * SUB] = jnp.where(sc, pos, -1)
        cstart_ref[:, s:s + 1] = carry.astype(I32)
        carry = carry + jnp.sum(scf, axis=1, keepdims=True)


def _topk(afft, cap):
    n_exp, ntok = afft.shape
    nsub = ntok // SUB
    return pl.pallas_call(
        functools.partial(_topk_kernel, cap=cap),
        out_shape=[
            jax.ShapeDtypeStruct((n_exp, ntok), I32),
            jax.ShapeDtypeStruct((n_exp, ntok), F32),
            jax.ShapeDtypeStruct((n_exp, nsub), I32),
        ],
        compiler_params=pltpu.CompilerParams(vmem_limit_bytes=VMEM_LIMIT_BYTES),
        name="topk",
    )(afft)


def _gather_kernel(cstart_ref, dense_ref, slot_ref, gsel_ref, h_ref, xe_ref, gs_ref, acc_ref, gacc_ref,
                   *, cap):
    e = pl.program_id(0)
    i = pl.program_id(1)
    n_sub = h_ref.shape[0] // SUB

    @pl.when(i == 0)
    def _():
        acc_ref[...] = jnp.zeros_like(acc_ref)
        gacc_ref[...] = jnp.zeros_like(gacc_ref)

    def compact(win):
        row = lax.broadcasted_iota(I32, (win, SUB), 0)
        for s in range(n_sub):
            cs = cstart_ref[e, i * n_sub + s]
            base = pl.multiple_of(lax.div(cs, 16) * 16, 16)
            sl = slot_ref[0, :, s * SUB:(s + 1) * SUB]
            hit = row == (sl - base)
            onehot = jnp.where(hit, 1.0, 0.0).astype(BF16)
            acc_ref[pl.ds(base, win), :] += _dot(onehot, h_ref[s * SUB:(s + 1) * SUB, :])
            g = gsel_ref[0, :, s * SUB:(s + 1) * SUB]
            gw = jnp.sum(jnp.where(hit, g, 0.0), axis=1, keepdims=True)
            gacc_ref[pl.ds(base, win), :] += jnp.broadcast_to(gw, (win, LANES))

    pl.when(dense_ref[e, i] == 0)(functools.partial(compact, GATHER_WIN_SPARSE))
    pl.when(dense_ref[e, i] != 0)(functools.partial(compact, GATHER_WIN))

    @pl.when(i == pl.num_programs(1) - 1)
    def _():
        xe_ref[0] = acc_ref[:cap].astype(BF16)
        gs_ref[0] = gacc_ref[:cap]


def _gather(cstart, slot3, gsel3, h2, cap):
    n_exp = slot3.shape[0]
    ntok, d = h2.shape
    tb = min(GATHER_TB, ntok)
    ends = jnp.concatenate([cstart[:, 1:], jnp.full((n_exp, 1), cap, I32)], axis=1)
    most = jnp.max((ends - cstart).reshape(n_exp, ntok // tb, tb // SUB), axis=-1)
    dense = (most > GATHER_WIN_SPARSE - 16).astype(I32)
    return pl.pallas_call(
        functools.partial(_gather_kernel, cap=cap),
        grid_spec=pltpu.PrefetchScalarGridSpec(
            num_scalar_prefetch=2,
            grid=(n_exp, ntok // tb),
            in_specs=[
                pl.BlockSpec((1, 1, tb), lambda e, i, cs, dn: (e, 0, i)),
                pl.BlockSpec((1, 1, tb), lambda e, i, cs, dn: (e, 0, i)),
                pl.BlockSpec((tb, d), lambda e, i, cs, dn: (i, 0)),
            ],
            out_specs=[
                pl.BlockSpec((1, cap, d), lambda e, i, cs, dn: (e, 0, 0)),
                pl.BlockSpec((1, cap, LANES), lambda e, i, cs, dn: (e, 0, 0)),
            ],
            scratch_shapes=[
                pltpu.VMEM((cap + GATHER_WIN, d), F32),
                pltpu.VMEM((cap + GATHER_WIN, LANES), F32),
            ],
        ),
        out_shape=[
            jax.ShapeDtypeStruct((n_exp, cap, d), BF16),
            jax.ShapeDtypeStruct((n_exp, cap, LANES), F32),
        ],
        compiler_params=_cparams(("arbitrary", "arbitrary")),
        name="gather",
    )(cstart, dense, slot3, gsel3, h2)


def _ffn_kernel(xe_ref, gs_ref, wg_ref, wu_ref, wd_ref, ye_ref, acc_ref):
    f = pl.program_id(1)
    cap = xe_ref.shape[1]
    rows = min(FFN_ROWS, cap)

    @pl.when(f == 0)
    def _():
        acc_ref[...] = jnp.zeros_like(acc_ref)

    wg = wg_ref[0].astype(BF16)
    wu = wu_ref[0].astype(BF16)
    wd = wd_ref[0].astype(BF16)
    for c in range(cap // rows):
        x = xe_ref[0, c * rows:(c + 1) * rows, :]
        g = _dot(x, wg)
        u = _dot(x, wu)
        hid = (g * jax.nn.sigmoid(g) * u).astype(BF16)
        acc_ref[c * rows:(c + 1) * rows, :] += _dot(hid, wd)

    @pl.when(f == pl.num_programs(1) - 1)
    def _():
        ye_ref[0] = (acc_ref[...] * gs_ref[0][:, :1]).astype(BF16)


def _ffn(xe, gs, w_gate, w_up, w_down):
    n_exp, cap, d = xe.shape
    ff = w_gate.shape[-1]
    tf = min(FFN_TF, ff)
    return pl.pallas_call(
        _ffn_kernel,
        grid=(n_exp, ff // tf),
        in_specs=[
            pl.BlockSpec((1, cap, d), lambda e, f: (e, 0, 0)),
            pl.BlockSpec((1, cap, LANES), lambda e, f: (e, 0, 0)),
            pl.BlockSpec((1, d, tf), lambda e, f: (e, 0, f)),
            pl.BlockSpec((1, d, tf), lambda e, f: (e, 0, f)),
            pl.BlockSpec((1, tf, d), lambda e, f: (e, f, 0)),
        ],
        out_specs=pl.BlockSpec((1, cap, d), lambda e, f: (e, 0, 0)),
        out_shape=jax.ShapeDtypeStruct((n_exp, cap, d), BF16),
        scratch_shapes=[pltpu.VMEM((cap, d), F32)],
        compiler_params=_cparams(("arbitrary", "arbitrary")),
        name="ffn",
    )(xe, gs, w_gate, w_up, w_down)


def _window_start(first_slot, cap, rows):
    return jnp.minimum(lax.div(first_slot, 16), (cap - rows) // 16) * 16


def _combine_kernel(cstart_ref, x1_ref, slot_ref, *refs, cap):
    *ye_refs, gfin_ref, y_ref = refs
    i = pl.program_id(0)
    n_sub = x1_ref.shape[0] // SUB
    rows = ye_refs[0].shape[1]
    win = min(COMBINE_WIN, rows)
    row = lax.broadcasted_iota(I32, (win, SUB), 0)
    dims = (((0,), (0,)), ((), ()))
    starts = [_window_start(cstart_ref[e, i * n_sub], cap, rows) for e in range(len(ye_refs))]
    for s in range(n_sub):
        out = x1_ref[s * SUB:(s + 1) * SUB, :]
        for e, ye_ref in enumerate(ye_refs):
            cs16 = lax.div(cstart_ref[e, i * n_sub + s], 16) * 16
            w0 = pl.multiple_of(jnp.clip(cs16 - starts[e], 0, rows - win), 16)
            sl = slot_ref[e, :, s * SUB:(s + 1) * SUB]
            onehot = jnp.where(row == (sl - (starts[e] + w0)), 1.0, 0.0).astype(BF16)
            out = out + lax.dot_general(onehot, ye_ref[0, pl.ds(w0, win), :], dims,
                                        preferred_element_type=F32)
        y_ref[s * SUB:(s + 1) * SUB, :] = (
            out * lax.rsqrt(jnp.mean(out * out, axis=-1, keepdims=True) + NORM_EPS) * gfin_ref[...])


def _combine_call(cstart, x1, slot3, ye, gfin, *, tb, rows):
    n_exp, cap, d = ye.shape
    ntok = x1.shape[0]
    n_sub = tb // SUB

    def window(e):
        return pl.BlockSpec((pl.Element(1), pl.Element(rows), pl.Element(d)),
                            lambda i, cs: (e, _window_start(cs[e, i * n_sub], cap, rows), 0))

    return pl.pallas_call(
        functools.partial(_combine_kernel, cap=cap),
        grid_spec=pltpu.PrefetchScalarGridSpec(
            num_scalar_prefetch=1,
            grid=(ntok // tb,),
            in_specs=[pl.BlockSpec((tb, d), lambda i, cs: (i, 0)),
                      pl.BlockSpec((n_exp, 1, tb), lambda i, cs: (0, 0, i))]
                     + [window(e) for e in range(n_exp)]
                     + [pl.BlockSpec((1, d), lambda i, cs: (0, 0))],
            out_specs=pl.BlockSpec((tb, d), lambda i, cs: (i, 0)),
        ),
        out_shape=jax.ShapeDtypeStruct((ntok, d), F32),
        compiler_params=_cparams(("arbitrary",)),
        name="combine",
    )(cstart, x1, slot3, *([ye] * n_exp), gfin)


def _combine(cstart, x1, slot3, ye, gfin):
    n_exp, cap, d = ye.shape
    ntok = x1.shape[0]
    tb = min(COMBINE_TB, ntok, cap // 2)
    n_sub = tb // SUB
    rows_dense = tb + 16
    rows_sparse = min(COMBINE_ROWS_SPARSE, rows_dense)
    assert rows_sparse >= SUB + 16 and rows_dense <= cap
    first = cstart[:, ::n_sub]
    used = jnp.concatenate([first[:, 1:], jnp.full((n_exp, 1), cap, I32)], axis=1) - first
    call = functools.partial(_combine_call, tb=tb)
    return lax.cond(jnp.max(used) <= rows_sparse - 16,
                    functools.partial(call, rows=rows_sparse),
                    functools.partial(call, rows=rows_dense),
                    cstart, x1, slot3, ye, gfin)


def _rope_tables(n):
    rows = n // GRID_W
    row = jnp.broadcast_to(jnp.arange(rows)[:, None], (rows, GRID_W)).reshape(-1).astype(F32)
    col = jnp.broadcast_to(jnp.arange(GRID_W)[None, :], (rows, GRID_W)).reshape(-1).astype(F32)
    half = HEAD_DIM // 2
    inv = ROPE_THETA ** (-jnp.arange(0, half, 2, dtype=F32) / half)
    ang_r = row[:, None] * inv[None, :]
    ang_c = col[:, None] * inv[None, :]
    cos = jnp.concatenate([jnp.cos(ang_r), jnp.cos(ang_r), jnp.cos(ang_c), jnp.cos(ang_c)], axis=-1)
    sin = jnp.concatenate([-jnp.sin(ang_r), jnp.sin(ang_r), -jnp.sin(ang_c), jnp.sin(ang_c)], axis=-1)
    return jnp.tile(cos, (1, 2)), jnp.tile(sin, (1, 2))


def _encoder(x, p):
    bsz, n, d = x.shape
    ntok = bsz * n
    n_exp = p["n_exp"]
    cap = CAPACITY_FACTOR * ntok // n_exp
    xt = x.reshape(ntok, d)
    cos, sin = _rope_tables(n)
    qa, ka, vat, qb, kb, vbt, stats = _proj(xt, p["gmix"], p["w_in"], p["gq"], p["gk"], cos, sin,
                                            p["seg"], n)

    qa_n, ka_n, qb_n, kb_n = jnp.sqrt(jnp.max(stats[:, :4, 0], axis=0)) * NORM_SLACK
    lanes = lambda v: jnp.broadcast_to(v, (1, LANES)).astype(F32)
    a_args = (lanes(ka_n), qa, ka, vat)
    a_out = lax.cond(qa_n * ka_n <= SAFE_RANGE,
                     functools.partial(_attn_a, bsz=bsz, n=n, fixed_shift=True),
                     functools.partial(_attn_a, bsz=bsz, n=n, fixed_shift=False), *a_args)
    tiles, d0, dstep = p["bias_tiles"](min(ATTN_TQ, n), min(ATTN_TK, n))
    b_args = (lanes(kb_n), qb, kb, vbt, tiles, p["lam"], p["gsub"])
    b_kw = dict(bsz=bsz, n=n, d0=d0, dstep=dstep)
    b_out = lax.cond(2.0 * qb_n * kb_n + p["bias_spread"] <= 2.0 * SAFE_RANGE,
                     functools.partial(_attn_b, fixed_shift=True, **b_kw),
                     functools.partial(_attn_b, fixed_shift=False, **b_kw), *b_args)

    x1, h2, afft = _outproj(xt, a_out, b_out, p["w_out"], p["gffn"], p["wr"], n_exp)
    slot, gsel, cstart = _topk(afft, cap)
    slot3 = slot.reshape(n_exp, 1, ntok)
    gsel3 = gsel.reshape(n_exp, 1, ntok)
    xe, gs = _gather(cstart, slot3, gsel3, h2, cap)
    ye = _ffn(xe, gs, p["w_gate"], p["w_up"], p["w_down"])
    y = _combine(cstart, x1, slot3, ye, p["gfin"])
    return y.reshape(bsz, n, d)


def kernel(x_prompt, x_sample, norm_mix, w_in, a_q_norm, a_k_norm, b_lambda, b_subln, w_out, rel_bias,
           norm_ffn, w_router, w_gate, w_up, w_down, norm_final):
    d = x_prompt.shape[-1]
    n_exp = w_router.shape[-1]
    wr = jnp.zeros((d, LANES), F32).at[:, :n_exp].set(w_router[0].astype(F32))
    wr_hi = wr.astype(BF16)
    lane_head = np.arange(LANES) // HEAD_DIM
    tile_cache = {}

    def bias_tiles(tq, tk):
        if (tq, tk) not in tile_cache:
            tile_cache[(tq, tk)] = _bias_tiles(rel_bias, tq, tk)
        return tile_cache[(tq, tk)]

    p = dict(
        n_exp=n_exp,
        gmix=norm_mix[0].astype(F32).reshape(1, d),
        w_in=w_in[0].astype(BF16),
        gq=jnp.tile(a_q_norm[0].astype(F32), 2).reshape(1, LANES),
        gk=jnp.tile(a_k_norm[0].astype(F32), 2).reshape(1, LANES),
        seg=jnp.asarray(lane_head[:, None] == lane_head[None, :], BF16),
        bias_tiles=bias_tiles,
        bias_spread=(jnp.max(rel_bias) - jnp.min(rel_bias)).astype(F32),
        lam=b_lambda[0].astype(F32),
        gsub=b_subln[0].astype(F32).reshape(1, LANES),
        w_out=w_out[0].astype(BF16),
        gffn=norm_ffn[0].astype(F32).reshape(1, d),
        wr=jnp.concatenate([wr_hi, (wr - wr_hi.astype(F32)).astype(BF16)], axis=1),
        w_gate=w_gate[0], w_up=w_up[0], w_down=w_down[0],
        gfin=norm_final.astype(F32).reshape(1, d),
    )
    return (_encoder(x_prompt, p), _encoder(x_sample, p))
```

```python
import functools
import math

import jax
import jax.numpy as jnp
import numpy as np
from jax import lax
from jax.experimental import pallas as pl
from jax.experimental.pallas import tpu as pltpu

F32 = jnp.float32
BF16 = jnp.bfloat16
I32 = jnp.int32

HEAD_DIM = 64
A_Q_HEADS = 8
A_KV_HEADS = 2
B_HEADS = 4
A_Q_COLS = A_Q_HEADS * HEAD_DIM
A_KV_COLS = A_KV_HEADS * HEAD_DIM
B_QK_COLS = B_HEADS * 2 * HEAD_DIM
B_V_DIM = 2 * HEAD_DIM
B_V_COLS = B_HEADS * B_V_DIM
GRID_W = 64
ROPE_THETA = 10000.0
REL_BUCKETS = 32
REL_MAX_DIST = 128
CAPACITY_FACTOR = 2
NORM_EPS = 1e-6
LAMBDA_INIT = 0.8 - 0.6 * math.exp(-0.3 * 0)
QK_SCALE = HEAD_DIM ** -0.5

LANES = 128
BF16_ROWS = 16
VMEM_LIMIT_BYTES = 56 * 1024 * 1024

TOKEN_TILE = 512
PROJ_ROW_CHUNKS = 1
OUTPROJ_ROW_CHUNKS = 2
ATTN_TQ = 512
ATTN_TK = 512
ATTN_A_TILES_PER_BODY = 8
ATTN_B_TILES_PER_BODY = 4
SUB = 128
GATHER_WIN = SUB + 16
GATHER_WIN_SPARSE = 64
GATHER_TB = 2048
GATHER_EXPERTS_PER_STEP = 2
COMBINE_TB = 512
COMBINE_WIN = 256
COMBINE_ROWS_SPARSE = 144
FFN_TF = 256
FFN_ROWS = 512
NEG_BIG = -1e30

SAFE_RANGE = 40.0
NORM_SLACK = 1.0 + 2.0 ** -7


def _dot(a, b):
    return jnp.dot(a, b, preferred_element_type=F32)


def _cparams(sem):
    return pltpu.CompilerParams(dimension_semantics=sem, vmem_limit_bytes=VMEM_LIMIT_BYTES)


def _split_dot(v, w_bf16):
    hi = v.astype(BF16)
    lo = (v - hi.astype(F32)).astype(BF16)
    return _dot(hi, w_bf16) + _dot(lo, w_bf16)


def _proj_kernel(x_ref, gmix_ref, w_ref, gq_ref, gk_ref, cos_ref, sin_ref, seg_ref,
                 qa_ref, ka_ref, vat_ref, qb_ref, kb_ref, vbt_ref, st_ref):
    seg = seg_ref[...]
    lane = lax.broadcasted_iota(I32, (1, LANES), 1)
    first_half = (lane % 32) < 16
    low_head = lane < HEAD_DIM
    ones_lane64 = jnp.where(lane == HEAD_DIM, 1.0, 0.0)

    def max_head_sumsq(vals):
        sq = vals.astype(F32)
        return jnp.max(_dot((sq * sq).astype(BF16), seg), axis=0, keepdims=True)

    def key_rows(k_pair, odd):
        kh = pltpu.roll(k_pair, HEAD_DIM, 1) if odd else k_pair
        return jnp.where(low_head, kh, ones_lane64).astype(BF16)

    def chunk(rows):
        x = x_ref[rows, :]
        h = (x * lax.rsqrt(jnp.mean(x * x, axis=-1, keepdims=True) + NORM_EPS) * gmix_ref[...]).astype(BF16)
        cos = cos_ref[rows, :]
        sin = sin_ref[rows, :]
        ones_row = jnp.where(lax.broadcasted_iota(I32, (BF16_ROWS, x.shape[0]), 0) == 0, 1.0, 0.0)

        def norm_rope(y, gain):
            ss = _split_dot(y * y, seg)
            y = y * lax.rsqrt(ss * (1.0 / HEAD_DIM) + NORM_EPS) * gain
            rot = jnp.where(first_half, pltpu.roll(y, LANES - 16, 1), pltpu.roll(y, 16, 1))
            return y * cos + rot * sin

        def value_rows_t(vt_rows):
            return jnp.concatenate([vt_rows, ones_row], axis=0).astype(BF16)

        c0 = 0
        aq = _dot(h, w_ref[:, c0:c0 + A_Q_COLS])
        qa_stat = jnp.zeros((1, LANES), F32)
        for c in range(A_Q_COLS // LANES):
            y = (norm_rope(aq[:, c * LANES:(c + 1) * LANES], gq_ref[...]) * QK_SCALE).astype(BF16)
            qa_ref[rows, c * LANES:(c + 1) * LANES] = y
            qa_stat = jnp.maximum(qa_stat, max_head_sumsq(y))
        c0 += A_Q_COLS

        akv = _dot(h, w_ref[:, c0:c0 + 2 * A_KV_COLS])
        k = norm_rope(akv[:, :LANES], gk_ref[...]).astype(BF16)
        ka_stat = max_head_sumsq(k)
        k = k.astype(F32)
        vt = akv[:, LANES:].T
        for g in range(A_KV_HEADS):
            ka_ref[g, rows, :] = key_rows(k, g)
            vat_ref[g, :, rows] = value_rows_t(vt[g * HEAD_DIM:(g + 1) * HEAD_DIM])
        c0 += 2 * A_KV_COLS

        qb = (_dot(h, w_ref[:, c0:c0 + B_QK_COLS]) * QK_SCALE).astype(BF16)
        qb_ref[rows, :] = qb
        c0 += B_QK_COLS
        bk = _dot(h, w_ref[:, c0:c0 + B_QK_COLS]).astype(BF16)
        c0 += B_QK_COLS
        bv = _dot(h, w_ref[:, c0:c0 + B_V_COLS])
        qb_stat = jnp.zeros((1, LANES), F32)
        kb_stat = jnp.zeros((1, LANES), F32)
        for hd in range(B_HEADS):
            blk = slice(hd * LANES, (hd + 1) * LANES)
            qb_stat = jnp.maximum(qb_stat, max_head_sumsq(qb[:, blk]))
            kb_stat = jnp.maximum(kb_stat, max_head_sumsq(bk[:, blk]))
            k_pair = bk[:, blk].astype(F32)
            kb_ref[hd, 0, rows, :] = key_rows(k_pair, 0)
            kb_ref[hd, 1, rows, :] = key_rows(k_pair, 1)
            vbt_ref[hd, :, rows] = value_rows_t(bv[:, blk].T)
        return qa_stat, ka_stat, qb_stat, kb_stat

    per_chunk = [chunk(rows) for rows in _row_chunks(x_ref.shape[0], PROJ_ROW_CHUNKS)]
    stats = [functools.reduce(jnp.maximum, s) for s in zip(*per_chunk)]
    stats = [jnp.broadcast_to(jnp.max(s, axis=1, keepdims=True), (1, LANES)) for s in stats]
    st_ref[0] = jnp.concatenate(stats + [jnp.zeros((8 - len(stats), LANES), F32)], axis=0)


def _proj(x, gmix, w_in, gq, gk, cos, sin, seg, n):
    ntok, d = x.shape
    tm = min(TOKEN_TILE, n)
    per_seq = n // tm
    ntiles = ntok // tm
    va_rows = HEAD_DIM + BF16_ROWS
    vb_rows = B_V_DIM + BF16_ROWS
    tok = lambda i: (i, 0)
    full = lambda i: (0, 0)
    return pl.pallas_call(
        _proj_kernel,
        grid=(ntiles,),
        in_specs=[
            pl.BlockSpec((tm, d), tok),
            pl.BlockSpec((1, d), full),
            pl.BlockSpec(w_in.shape, full),
            pl.BlockSpec((1, LANES), full),
            pl.BlockSpec((1, LANES), full),
            pl.BlockSpec((tm, LANES), lambda i: (i % per_seq, 0)),
            pl.BlockSpec((tm, LANES), lambda i: (i % per_seq, 0)),
            pl.BlockSpec((LANES, LANES), full),
        ],
        out_specs=[
            pl.BlockSpec((tm, A_Q_COLS), tok),
            pl.BlockSpec((A_KV_HEADS, tm, LANES), lambda i: (0, i, 0)),
            pl.BlockSpec((A_KV_HEADS, va_rows, tm), lambda i: (0, 0, i)),
            pl.BlockSpec((tm, B_QK_COLS), tok),
            pl.BlockSpec((B_HEADS, 2, tm, LANES), lambda i: (0, 0, i, 0)),
            pl.BlockSpec((B_HEADS, vb_rows, tm), lambda i: (0, 0, i)),
            pl.BlockSpec((1, 8, LANES), lambda i: (i, 0, 0)),
        ],
        out_shape=[
            jax.ShapeDtypeStruct((ntok, A_Q_COLS), BF16),
            jax.ShapeDtypeStruct((A_KV_HEADS, ntok, LANES), BF16),
            jax.ShapeDtypeStruct((A_KV_HEADS, va_rows, ntok), BF16),
            jax.ShapeDtypeStruct((ntok, B_QK_COLS), BF16),
            jax.ShapeDtypeStruct((B_HEADS, 2, ntok, LANES), BF16),
            jax.ShapeDtypeStruct((B_HEADS, vb_rows, ntok), BF16),
            jax.ShapeDtypeStruct((ntiles, 8, LANES), F32),
        ],
        compiler_params=_cparams(("arbitrary",)),
        name="proj",
    )(x, gmix, w_in, gq, gk, cos, sin, seg)


def _query_operands(q_pair, kmax, fixed_shift):
    qt = q_pair.astype(F32).T
    tq = qt.shape[1]
    first_row = lax.broadcasted_iota(I32, (HEAD_DIM, tq), 0) == 0
    out = []
    for half in (qt[:HEAD_DIM], qt[HEAD_DIM:]):
        tail = jnp.zeros((HEAD_DIM, tq), F32)
        if fixed_shift:
            shift = jnp.sqrt(jnp.sum(half * half, axis=0, keepdims=True)) * kmax
            tail = jnp.where(first_row, -shift, tail)
        out.append(jnp.concatenate([half, tail], axis=0).astype(BF16))
    return out


def _online_step(k, qt, vt, state, bias=None):
    m, acc = state
    s = _dot(k, qt)
    if bias is not None:
        s = s + bias
    m_new = jnp.maximum(m, jnp.max(s, axis=0, keepdims=True))
    acc = jnp.exp(m - m_new) * acc + _dot(vt, jnp.exp(s - m_new).astype(BF16))
    return m_new, acc


def _tiles_per_body(nk, preferred):
    per_body = preferred
    while nk % per_body:
        per_body //= 2
    assert per_body >= 2 and per_body % 2 == 0, "the two-slot score pipeline walks key tiles in pairs"
    return per_body


def _two_slot_pipeline(nk, s_ref, score_fn, weigh_fn, accs, per_body):
    n_maps = s_ref.shape[1]

    def put(slot, j):
        for h, s in enumerate(score_fn(j)):
            s_ref[slot, h] = s

    def take(slot, j, accs):
        return weigh_fn(j, tuple(s_ref[slot, h] for h in range(n_maps)), accs)

    def run(first, count, accs):
        for t in range(count):
            put((t + 1) % 2, first + t + 1)
            accs = take(t % 2, first + t, accs)
        return accs

    put(0, 0)
    accs = lax.fori_loop(0, nk // per_body - 1,
                         lambda i, accs: run(per_body * i, per_body, accs), accs)
    accs = run(nk - per_body, per_body - 1, accs)
    return take((per_body - 1) % 2, nk - 1, accs)


def _score_scratch(fixed_shift, n_maps, tq, tk):
    return [pltpu.VMEM((2, n_maps, tk, tq), F32)] if fixed_shift else []


def _attn_a_kernel(kmax_ref, q_ref, k_ref, vt_ref, o_ref, *scratch, tk, fixed_shift):
    tq = q_ref.shape[0]
    nk = k_ref.shape[1] // tk
    n_pairs = q_ref.shape[1] // LANES
    kmax = kmax_ref[:, :1]
    qts = []
    for pr in range(n_pairs):
        qts += _query_operands(q_ref[:, pr * LANES:(pr + 1) * LANES], kmax, fixed_shift)

    def tiles(j):
        off = pl.multiple_of(j * tk, tk)
        return k_ref[0, pl.ds(off, tk), :], vt_ref[0, :, pl.ds(off, tk)]

    zero = jnp.zeros((vt_ref.shape[1], tq), F32)
    if fixed_shift:
        def scores(j):
            k = tiles(j)[0]
            return tuple(_dot(k, qt) for qt in qts)

        def weighted(j, ss, accs):
            vt = tiles(j)[1]
            return tuple(acc + _dot(vt, jnp.exp(s).astype(BF16)) for s, acc in zip(ss, accs))

        accs = _two_slot_pipeline(nk, scratch[0], scores, weighted, tuple(zero for _ in qts),
                                  _tiles_per_body(nk, ATTN_A_TILES_PER_BODY))
    else:
        def body(j, states):
            k, vt = tiles(j)
            return tuple(_online_step(k, qt, vt, st) for qt, st in zip(qts, states))
        init = (jnp.full((1, tq), NEG_BIG, F32), zero)
        accs = [st[1] for st in lax.fori_loop(0, nk, body, tuple(init for _ in qts))]

    for pr in range(n_pairs):
        halves = [acc[:HEAD_DIM] / acc[HEAD_DIM:HEAD_DIM + 1] for acc in accs[2 * pr:2 * pr + 2]]
        o_ref[:, pr * LANES:(pr + 1) * LANES] = jnp.concatenate(halves, axis=0).T.astype(BF16)


def _attn_a(kmax, qa, ka, vat, *, bsz, n, fixed_shift):
    ntok = qa.shape[0]
    tq = min(ATTN_TQ, n)
    tk = min(ATTN_TK, n)
    nq = n // tq
    gcols = A_Q_COLS // A_KV_HEADS
    return pl.pallas_call(
        functools.partial(_attn_a_kernel, tk=tk, fixed_shift=fixed_shift),
        grid=(bsz, A_KV_HEADS, nq),
        in_specs=[
            pl.BlockSpec((1, LANES), lambda b, g, i: (0, 0)),
            pl.BlockSpec((tq, gcols), lambda b, g, i: (b * nq + i, g)),
            pl.BlockSpec((1, n, LANES), lambda b, g, i: (g, b, 0)),
            pl.BlockSpec((1, vat.shape[1], n), lambda b, g, i: (g, 0, b)),
        ],
        out_specs=pl.BlockSpec((tq, gcols), lambda b, g, i: (b * nq + i, g)),
        out_shape=jax.ShapeDtypeStruct((ntok, A_Q_COLS), BF16),
        scratch_shapes=_score_scratch(fixed_shift, A_Q_HEADS // A_KV_HEADS, tq, tk),
        compiler_params=_cparams(("arbitrary", "arbitrary", "arbitrary")),
        name="attn_a_shift" if fixed_shift else "attn_a_online",
    )(kmax, qa, ka, vat)


def _attn_b_kernel(kmax_ref, q_ref, k_ref, vt_ref, diag_ref, lam_ref, gsub_ref, o_ref, bias_ref, *scratch,
                   tk, d0, dstep, fixed_shift):
    tq = q_ref.shape[0]
    nk = k_ref.shape[2] // tk
    n_tiles = bias_ref.shape[0]
    qi = pl.program_id(2)
    qts = _query_operands(q_ref[...], kmax_ref[:, :1], fixed_shift)

    @pl.when(qi == 0)
    def _():
        for t in range(n_tiles):
            by_offset = jnp.broadcast_to(diag_ref[0, t], (tk, diag_ref.shape[-1]))
            bias_ref[t] = pltpu.roll(by_offset, 0, 1, stride=1, stride_axis=0)[:, :tq]

    def tiles(j):
        off = pl.multiple_of(j * tk, tk)
        return (k_ref[0, 0, pl.ds(off, tk), :], k_ref[0, 1, pl.ds(off, tk), :]), vt_ref[0, :, pl.ds(off, tk)]

    def bias_tile(j):
        u = jnp.clip(j * tk - qi * tq - d0 + dstep, 0, (n_tiles - 1) * dstep)
        return bias_ref[lax.div(u, dstep)]

    zero = jnp.zeros((vt_ref.shape[1], tq), F32)
    if fixed_shift:
        def scores(j):
            return tuple(_dot(k, qt) for k, qt in zip(tiles(j)[0], qts))

        def weighted(j, ss, accs):
            vt = tiles(j)[1]
            bias = bias_tile(j)
            return tuple(acc + _dot(vt, jnp.exp(s + bias).astype(BF16)) for s, acc in zip(ss, accs))

        accs = _two_slot_pipeline(nk, scratch[0], scores, weighted, (zero, zero),
                                  _tiles_per_body(nk, ATTN_B_TILES_PER_BODY))
    else:
        def body(j, states):
            ks, vt = tiles(j)
            bias = bias_tile(j)
            return tuple(_online_step(k, qt, vt, st, bias) for k, qt, st in zip(ks, qts, states))
        init = (jnp.full((1, tq), NEG_BIG, F32), zero)
        accs = [st[1] for st in lax.fori_loop(0, nk, body, (init, init))]

    o1, o2 = [acc[:B_V_DIM] / acc[B_V_DIM:B_V_DIM + 1] for acc in accs]
    lp = lam_ref[...]
    lam = (jnp.exp(jnp.sum(lp[0:1] * lp[1:2], axis=-1, keepdims=True))
           - jnp.exp(jnp.sum(lp[2:3] * lp[3:4], axis=-1, keepdims=True)) + LAMBDA_INIT)
    out = (o1 - lam * o2).T
    out = out * lax.rsqrt(jnp.mean(out * out, axis=-1, keepdims=True) + NORM_EPS) * gsub_ref[...]
    o_ref[...] = (out * (1.0 - LAMBDA_INIT)).astype(BF16)


def _t5_bucket(rel):
    nb = REL_BUCKETS // 2
    ret = (rel > 0).astype(I32) * nb
    n = jnp.abs(rel)
    max_exact = nb // 2
    nf = jnp.maximum(n, 1).astype(F32)
    large = max_exact + (jnp.log(nf / max_exact) / math.log(REL_MAX_DIST / max_exact)
                         * (nb - max_exact)).astype(I32)
    large = jnp.minimum(large, nb - 1)
    return ret + jnp.where(n < max_exact, n, large)


def _bias_diagonals(rel_bias, tq, tk):
    dstep = math.gcd(tq, tk)
    reach = (tq + tk + REL_MAX_DIST) // dstep
    band = [d for d in range(-reach * dstep, reach * dstep + 1, dstep)
            if d + tk - 1 > -REL_MAX_DIST and d - (tq - 1) < REL_MAX_DIST]
    far = tq + tk + REL_MAX_DIST
    ds = jnp.asarray([-far] + band + [far], I32)
    span = tq + tk
    m = jnp.arange(span, dtype=I32)
    q_minus_k = jnp.where(m < tq, m, m - span)
    bucket = _t5_bucket(ds[:, None] - q_minus_k[None, :])
    table = rel_bias.astype(F32)
    table = table - jnp.max(table, axis=0, keepdims=True)
    diag = jnp.moveaxis(table[bucket], -1, 0)
    return diag[:, :, None, :], band[0], dstep


def _attn_b(kmax, qb, kb, vbt, diag, lam_params, gsub, *, bsz, n, tq, tk, d0, dstep, fixed_shift):
    ntok = qb.shape[0]
    nq = n // tq
    return pl.pallas_call(
        functools.partial(_attn_b_kernel, tk=tk, d0=d0, dstep=dstep, fixed_shift=fixed_shift),
        grid=(bsz, B_HEADS, nq),
        in_specs=[
            pl.BlockSpec((1, LANES), lambda b, h, i: (0, 0)),
            pl.BlockSpec((tq, LANES), lambda b, h, i: (b * nq + i, h)),
            pl.BlockSpec((1, 2, n, LANES), lambda b, h, i: (h, 0, b, 0)),
            pl.BlockSpec((1, vbt.shape[1], n), lambda b, h, i: (h, 0, b)),
            pl.BlockSpec((1,) + diag.shape[1:], lambda b, h, i: (h, 0, 0, 0)),
            pl.BlockSpec(lam_params.shape, lambda b, h, i: (0, 0)),
            pl.BlockSpec((1, LANES), lambda b, h, i: (0, 0)),
        ],
        out_specs=pl.BlockSpec((tq, LANES), lambda b, h, i: (b * nq + i, h)),
        out_shape=jax.ShapeDtypeStruct((ntok, B_V_COLS), BF16),
        scratch_shapes=[pltpu.VMEM((diag.shape[1], tk, tq), F32)] + _score_scratch(fixed_shift, 2, tq, tk),
        compiler_params=_cparams(("arbitrary", "arbitrary", "arbitrary")),
        name="attn_b_shift" if fixed_shift else "attn_b_online",
    )(kmax, qb, kb, vbt, diag, lam_params, gsub)


def _outproj_kernel(x_ref, a_ref, b_ref, wo_ref, gffn_ref, wr_ref,
                    x1_ref, h2_ref, afft_ref, *, n_experts):
    wa = wo_ref.shape[0] // 2
    lane = lax.broadcasted_iota(I32, (1, LANES), 1)
    for rows in _row_chunks(x_ref.shape[0], OUTPROJ_ROW_CHUNKS):
        x1 = x_ref[rows, :] + _dot(a_ref[rows, :], wo_ref[:wa]) + _dot(b_ref[rows, :], wo_ref[wa:])
        x1_ref[rows, :] = x1
        h2 = x1 * lax.rsqrt(jnp.mean(x1 * x1, axis=-1, keepdims=True) + NORM_EPS) * gffn_ref[...]
        hi = h2.astype(BF16)
        h2_ref[rows, :] = hi
        lo = (h2 - hi.astype(F32)).astype(BF16)
        both = _dot(hi, wr_ref[...])
        logits = both[:, :LANES] + both[:, LANES:] + _dot(lo, wr_ref[:, :LANES])
        logits = jnp.where(lane < n_experts, logits, NEG_BIG)
        ex = jnp.exp(logits - jnp.max(logits, axis=-1, keepdims=True))
        aff = ex / jnp.sum(ex, axis=-1, keepdims=True)
        afft_ref[:, rows] = aff.T[:n_experts]


def _row_chunks(n_rows, n_chunks):
    chunk = max(n_rows // n_chunks, LANES)
    return [slice(r, r + chunk) for r in range(0, n_rows, chunk)]


def _outproj(x, a_out, b_out, w_out, gffn, wr, n_experts):
    ntok, d = x.shape
    tm = min(TOKEN_TILE, ntok)
    tok = lambda i: (i, 0)
    full = lambda i: (0, 0)
    return pl.pallas_call(
        functools.partial(_outproj_kernel, n_experts=n_experts),
        grid=(ntok // tm,),
        in_specs=[
            pl.BlockSpec((tm, d), tok),
            pl.BlockSpec((tm, a_out.shape[1]), tok),
            pl.BlockSpec((tm, b_out.shape[1]), tok),
            pl.BlockSpec(w_out.shape, full),
            pl.BlockSpec((1, d), full),
            pl.BlockSpec(wr.shape, full),
        ],
        out_specs=[
            pl.BlockSpec((tm, d), tok),
            pl.BlockSpec((tm, d), tok),
            pl.BlockSpec((n_experts, tm), lambda i: (0, i)),
        ],
        out_shape=[
            jax.ShapeDtypeStruct((ntok, d), F32),
            jax.ShapeDtypeStruct((ntok, d), BF16),
            jax.ShapeDtypeStruct((n_experts, ntok), F32),
        ],
        compiler_params=_cparams(("arbitrary",)),
        name="outproj",
    )(x, a_out, b_out, w_out, gffn, wr)


def _topk_kernel(aff_ref, slot_ref, gsel_ref, cstart_ref, *, cap):
    aff = aff_ref[...]
    n_exp, ntok = aff.shape

    def count(mask):
        return jnp.sum(mask.astype(I32), axis=1, keepdims=True)

    def as_float(bits):
        return lax.bitcast_convert_type(bits, np.float32)

    def value_bit(i, t):
        cand = t | lax.shift_left(jnp.int32(1), 30 - i)
        return jnp.where(count(aff >= as_float(cand)) >= cap, cand, t)

    thr = lax.fori_loop(0, 31, value_bit, jnp.zeros((n_exp, 1), I32))
    above = aff >= as_float(thr + 1)
    tie = (aff >= as_float(thr)) & jnp.logical_not(above)
    need = cap - count(above)
    idx = lax.broadcasted_iota(I32, aff.shape, 1)

    n_bits = int(ntok).bit_length()

    def index_bit(i, bound):
        cand = bound + lax.shift_left(jnp.int32(1), n_bits - 1 - i)
        ok = (cand <= ntok) & (count(tie & (idx < cand)) <= need)
        return jnp.where(ok, cand, bound)

    bound = lax.fori_loop(0, n_bits, index_bit, jnp.zeros((n_exp, 1), I32))
    sel = above | (tie & (idx < bound))
    gsel_ref[...] = jnp.where(sel, aff, 0.0)

    r = lax.broadcasted_iota(I32, (SUB, SUB), 0)
    c = lax.broadcasted_iota(I32, (SUB, SUB), 1)
    before = jnp.where(r < c, 1.0, 0.0).astype(BF16)
    carry = jnp.zeros((n_exp, 1), F32)
    n_sub = ntok // SUB
    for s in range(n_sub):
        sc = sel[:, s * SUB:(s + 1) * SUB]
        scf = jnp.where(sc, 1.0, 0.0)
        pos = (_dot(scf.astype(BF16), before) + carry).astype(I32)
        slot_ref[:, s * SUB:(s + 1) * SUB] = jnp.where(sc, pos, -1)
        cstart_ref[:, s:s + 1] = carry.astype(I32)
        carry = carry + jnp.sum(scf, axis=1, keepdims=True)


def _topk(afft, cap):
    n_exp, ntok = afft.shape
    nsub = ntok // SUB
    return pl.pallas_call(
        functools.partial(_topk_kernel, cap=cap),
        out_shape=[
            jax.ShapeDtypeStruct((n_exp, ntok), I32),
            jax.ShapeDtypeStruct((n_exp, ntok), F32),
            jax.ShapeDtypeStruct((n_exp, nsub), I32),
        ],
        compiler_params=pltpu.CompilerParams(vmem_limit_bytes=VMEM_LIMIT_BYTES),
        name="topk",
    )(afft)


def _gather_kernel(cstart_ref, dense_ref, slot_ref, gsel_ref, h_ref, xe_ref, gs_ref):
    i = pl.program_id(1)
    n_sub = h_ref.shape[0] // SUB
    per_step = xe_ref.shape[0]

    @pl.when(i == 0)
    def _():
        xe_ref[...] = jnp.zeros_like(xe_ref)
        gs_ref[...] = jnp.zeros_like(gs_ref)

    def compact(x, win):
        e = pl.program_id(0) * per_step + x
        row = lax.broadcasted_iota(I32, (win, SUB), 0)
        for s in range(n_sub):
            cs = cstart_ref[e, i * n_sub + s]
            rows = pl.ds(pl.multiple_of(lax.div(cs, 16) * 16, 16), win)
            sl = slot_ref[x, :, s * SUB:(s + 1) * SUB]
            hit = row == (sl - rows.start)
            onehot = jnp.where(hit, 1.0, 0.0).astype(BF16)
            moved = _dot(onehot, h_ref[s * SUB:(s + 1) * SUB, :])
            xe_ref[x, rows, :] = (xe_ref[x, rows, :].astype(F32) + moved).astype(BF16)
            g = gsel_ref[x, :, s * SUB:(s + 1) * SUB]
            gw = jnp.sum(jnp.where(hit, g, 0.0), axis=1, keepdims=True)
            gs_ref[x, rows, :] += jnp.broadcast_to(gw, (win, LANES))

    for x in range(per_step):
        dense = dense_ref[pl.program_id(0) * per_step + x, i]
        pl.when(dense == 0)(functools.partial(compact, x, GATHER_WIN_SPARSE))
        pl.when(dense != 0)(functools.partial(compact, x, GATHER_WIN))


def _gather(cstart, slot3, gsel3, h2, cap):
    n_exp = slot3.shape[0]
    ntok, d = h2.shape
    tb = min(GATHER_TB, ntok)
    ends = jnp.concatenate([cstart[:, 1:], jnp.full((n_exp, 1), cap, I32)], axis=1)
    most = jnp.max((ends - cstart).reshape(n_exp, ntok // tb, tb // SUB), axis=-1)
    dense = (most > GATHER_WIN_SPARSE - 16).astype(I32)
    per_step = math.gcd(GATHER_EXPERTS_PER_STEP, n_exp)
    rows = cap + GATHER_WIN
    return pl.pallas_call(
        _gather_kernel,
        grid_spec=pltpu.PrefetchScalarGridSpec(
            num_scalar_prefetch=2,
            grid=(n_exp // per_step, ntok // tb),
            in_specs=[
                pl.BlockSpec((per_step, 1, tb), lambda e, i, cs, dn: (e, 0, i)),
                pl.BlockSpec((per_step, 1, tb), lambda e, i, cs, dn: (e, 0, i)),
                pl.BlockSpec((tb, d), lambda e, i, cs, dn: (i, 0)),
            ],
            out_specs=[
                pl.BlockSpec((per_step, rows, d), lambda e, i, cs, dn: (e, 0, 0)),
                pl.BlockSpec((per_step, rows, LANES), lambda e, i, cs, dn: (e, 0, 0)),
            ],
        ),
        out_shape=[
            jax.ShapeDtypeStruct((n_exp, rows, d), BF16),
            jax.ShapeDtypeStruct((n_exp, rows, LANES), F32),
        ],
        compiler_params=_cparams(("arbitrary", "arbitrary")),
        name="gather",
    )(cstart, dense, slot3, gsel3, h2)


def _ffn_kernel(xe_ref, gs_ref, wg_ref, wu_ref, wd_ref, ye_ref, acc_ref):
    f = pl.program_id(1)
    cap = xe_ref.shape[1]
    rows = min(FFN_ROWS, cap)

    @pl.when(f == 0)
    def _():
        acc_ref[...] = jnp.zeros_like(acc_ref)

    wg = wg_ref[0].astype(BF16)
    wu = wu_ref[0].astype(BF16)
    wd = wd_ref[0].astype(BF16)
    for c in range(cap // rows):
        x = xe_ref[0, c * rows:(c + 1) * rows, :]
        g = _dot(x, wg)
        u = _dot(x, wu)
        hid = (g * jax.nn.sigmoid(g) * u).astype(BF16)
        acc_ref[c * rows:(c + 1) * rows, :] += _dot(hid, wd)

    @pl.when(f == pl.num_programs(1) - 1)
    def _():
        ye_ref[0] = (acc_ref[...] * gs_ref[0][:, :1]).astype(BF16)


def _ffn(xe, gs, w_gate, w_up, w_down, cap):
    n_exp, _, d = xe.shape
    ff = w_gate.shape[-1]
    tf = min(FFN_TF, ff)
    return pl.pallas_call(
        _ffn_kernel,
        grid=(n_exp, ff // tf),
        in_specs=[
            pl.BlockSpec((1, cap, d), lambda e, f: (e, 0, 0)),
            pl.BlockSpec((1, cap, LANES), lambda e, f: (e, 0, 0)),
            pl.BlockSpec((1, d, tf), lambda e, f: (e, 0, f)),
            pl.BlockSpec((1, d, tf), lambda e, f: (e, 0, f)),
            pl.BlockSpec((1, tf, d), lambda e, f: (e, f, 0)),
        ],
        out_specs=pl.BlockSpec((1, cap, d), lambda e, f: (e, 0, 0)),
        out_shape=jax.ShapeDtypeStruct((n_exp, cap, d), BF16),
        scratch_shapes=[pltpu.VMEM((cap, d), F32)],
        compiler_params=_cparams(("arbitrary", "arbitrary")),
        name="ffn",
    )(xe, gs, w_gate, w_up, w_down)


def _window_start(first_slot, cap, rows):
    return jnp.minimum(lax.div(first_slot, 16), (cap - rows) // 16) * 16


def _combine_kernel(cstart_ref, x1_ref, slot_ref, *refs, cap):
    *ye_refs, gfin_ref, y_ref = refs
    i = pl.program_id(0)
    n_sub = x1_ref.shape[0] // SUB
    rows = ye_refs[0].shape[1]
    win = min(COMBINE_WIN, rows)
    row = lax.broadcasted_iota(I32, (win, SUB), 0)
    dims = (((0,), (0,)), ((), ()))
    starts = [_window_start(cstart_ref[e, i * n_sub], cap, rows) for e in range(len(ye_refs))]
    for s in range(n_sub):
        out = x1_ref[s * SUB:(s + 1) * SUB, :]
        for e, ye_ref in enumerate(ye_refs):
            cs16 = lax.div(cstart_ref[e, i * n_sub + s], 16) * 16
            w0 = pl.multiple_of(jnp.clip(cs16 - starts[e], 0, rows - win), 16)
            sl = slot_ref[e, :, s * SUB:(s + 1) * SUB]
            onehot = jnp.where(row == (sl - (starts[e] + w0)), 1.0, 0.0).astype(BF16)
            out = out + lax.dot_general(onehot, ye_ref[0, pl.ds(w0, win), :], dims,
                                        preferred_element_type=F32)
        y_ref[s * SUB:(s + 1) * SUB, :] = (
            out * lax.rsqrt(jnp.mean(out * out, axis=-1, keepdims=True) + NORM_EPS) * gfin_ref[...])


def _combine_call(cstart, x1, slot3, ye, gfin, *, tb, rows):
    n_exp, cap, d = ye.shape
    ntok = x1.shape[0]
    n_sub = tb // SUB

    def window(e):
        return pl.BlockSpec((pl.Element(1), pl.Element(rows), pl.Element(d)),
                            lambda i, cs: (e, _window_start(cs[e, i * n_sub], cap, rows), 0))

    return pl.pallas_call(
        functools.partial(_combine_kernel, cap=cap),
        grid_spec=pltpu.PrefetchScalarGridSpec(
            num_scalar_prefetch=1,
            grid=(ntok // tb,),
            in_specs=[pl.BlockSpec((tb, d), lambda i, cs: (i, 0)),
                      pl.BlockSpec((n_exp, 1, tb), lambda i, cs: (0, 0, i))]
                     + [window(e) for e in range(n_exp)]
                     + [pl.BlockSpec((1, d), lambda i, cs: (0, 0))],
            out_specs=pl.BlockSpec((tb, d), lambda i, cs: (i, 0)),
        ),
        out_shape=jax.ShapeDtypeStruct((ntok, d), F32),
        compiler_params=_cparams(("arbitrary",)),
        name="combine",
    )(cstart, x1, slot3, *([ye] * n_exp), gfin)


def _combine(cstart, x1, slot3, ye, gfin):
    n_exp, cap, d = ye.shape
    ntok = x1.shape[0]
    tb = min(COMBINE_TB, ntok, cap // 2)
    n_sub = tb // SUB
    rows_dense = tb + 16
    rows_sparse = min(COMBINE_ROWS_SPARSE, rows_dense)
    assert rows_sparse >= SUB + 16 and rows_dense <= cap
    first = cstart[:, ::n_sub]
    used = jnp.concatenate([first[:, 1:], jnp.full((n_exp, 1), cap, I32)], axis=1) - first
    call = functools.partial(_combine_call, tb=tb)
    return lax.cond(jnp.max(used) <= rows_sparse - 16,
                    functools.partial(call, rows=rows_sparse),
                    functools.partial(call, rows=rows_dense),
                    cstart, x1, slot3, ye, gfin)


def _rope_tables(n):
    rows = n // GRID_W
    row = jnp.broadcast_to(jnp.arange(rows)[:, None], (rows, GRID_W)).reshape(-1).astype(F32)
    col = jnp.broadcast_to(jnp.arange(GRID_W)[None, :], (rows, GRID_W)).reshape(-1).astype(F32)
    half = HEAD_DIM // 2
    inv = ROPE_THETA ** (-jnp.arange(0, half, 2, dtype=F32) / half)
    ang_r = row[:, None] * inv[None, :]
    ang_c = col[:, None] * inv[None, :]
    cos = jnp.concatenate([jnp.cos(ang_r), jnp.cos(ang_r), jnp.cos(ang_c), jnp.cos(ang_c)], axis=-1)
    sin = jnp.concatenate([-jnp.sin(ang_r), jnp.sin(ang_r), -jnp.sin(ang_c), jnp.sin(ang_c)], axis=-1)
    return jnp.tile(cos, (1, 2)), jnp.tile(sin, (1, 2))


def _encoder(x, p):
    bsz, n, d = x.shape
    ntok = bsz * n
    n_exp = p["n_exp"]
    cap = CAPACITY_FACTOR * ntok // n_exp
    xt = x.reshape(ntok, d)
    cos, sin = _rope_tables(n)
    qa, ka, vat, qb, kb, vbt, stats = _proj(xt, p["gmix"], p["w_in"], p["gq"], p["gk"], cos, sin,
                                            p["seg"], n)

    qa_n, ka_n, qb_n, kb_n = jnp.sqrt(jnp.max(stats[:, :4, 0], axis=0)) * NORM_SLACK
    lanes = lambda v: jnp.broadcast_to(v, (1, LANES)).astype(F32)
    a_args = (lanes(ka_n), qa, ka, vat)
    a_out = lax.cond(qa_n * ka_n <= SAFE_RANGE,
                     functools.partial(_attn_a, bsz=bsz, n=n, fixed_shift=True),
                     functools.partial(_attn_a, bsz=bsz, n=n, fixed_shift=False), *a_args)
    tq, tk = min(ATTN_TQ, n), min(ATTN_TK, n)
    diag, d0, dstep = _bias_diagonals(p["rel_bias"], tq, tk)
    b_args = (lanes(kb_n), qb, kb, vbt, diag, p["lam"], p["gsub"])
    b_kw = dict(bsz=bsz, n=n, tq=tq, tk=tk, d0=d0, dstep=dstep)
    b_out = lax.cond(2.0 * qb_n * kb_n + p["bias_spread"] <= 2.0 * SAFE_RANGE,
                     functools.partial(_attn_b, fixed_shift=True, **b_kw),
                     functools.partial(_attn_b, fixed_shift=False, **b_kw), *b_args)

    x1, h2, afft = _outproj(xt, a_out, b_out, p["w_out"], p["gffn"], p["wr"], n_exp)
    slot, gsel, cstart = _topk(afft, cap)
    slot3 = slot.reshape(n_exp, 1, ntok)
    gsel3 = gsel.reshape(n_exp, 1, ntok)
    xe, gs = _gather(cstart, slot3, gsel3, h2, cap)
    ye = _ffn(xe, gs, p["w_gate"], p["w_up"], p["w_down"], cap)
    y = _combine(cstart, x1, slot3, ye, p["gfin"])
    return y.reshape(bsz, n, d)


def kernel(x_prompt, x_sample, norm_mix, w_in, a_q_norm, a_k_norm, b_lambda, b_subln, w_out, rel_bias,
           norm_ffn, w_router, w_gate, w_up, w_down, norm_final):
    d = x_prompt.shape[-1]
    n_exp = w_router.shape[-1]
    wr = jnp.zeros((d, LANES), F32).at[:, :n_exp].set(w_router[0].astype(F32))
    wr_hi = wr.astype(BF16)
    lane_head = np.arange(LANES) // HEAD_DIM
    p = dict(
        n_exp=n_exp,
        gmix=norm_mix[0].astype(F32).reshape(1, d),
        w_in=w_in[0].astype(BF16),
        gq=jnp.tile(a_q_norm[0].astype(F32), 2).reshape(1, LANES),
        gk=jnp.tile(a_k_norm[0].astype(F32), 2).reshape(1, LANES),
        seg=jnp.asarray(lane_head[:, None] == lane_head[None, :], BF16),
        rel_bias=rel_bias,
        bias_spread=(jnp.max(rel_bias) - jnp.min(rel_bias)).astype(F32),
        lam=b_lambda[0].astype(F32),
        gsub=b_subln[0].astype(F32).reshape(1, LANES),
        w_out=w_out[0].astype(BF16),
        gffn=norm_ffn[0].astype(F32).reshape(1, d),
        wr=jnp.concatenate([wr_hi, (wr - wr_hi.astype(F32)).astype(BF16)], axis=1),
        w_gate=w_gate[0], w_up=w_up[0], w_down=w_down[0],
        gfin=norm_final.astype(F32).reshape(1, d),
    )
    return (_encoder(x_prompt, p), _encoder(x_sample, p))
```

```python
import functools
import math

import jax
import jax.numpy as jnp
import numpy as np
from jax import lax
from jax.experimental import pallas as pl
from jax.experimental.pallas import tpu as pltpu

F32 = jnp.float32
BF16 = jnp.bfloat16
I32 = jnp.int32

HEAD_DIM = 64
A_Q_HEADS = 8
A_KV_HEADS = 2
B_HEADS = 4
A_Q_COLS = A_Q_HEADS * HEAD_DIM
A_KV_COLS = A_KV_HEADS * HEAD_DIM
B_QK_COLS = B_HEADS * 2 * HEAD_DIM
B_V_DIM = 2 * HEAD_DIM
B_V_COLS = B_HEADS * B_V_DIM
GRID_W = 64
ROPE_THETA = 10000.0
REL_BUCKETS = 32
REL_MAX_DIST = 128
CAPACITY_FACTOR = 2
NORM_EPS = 1e-6
LAMBDA_INIT = 0.8 - 0.6 * math.exp(-0.3 * 0)
QK_SCALE = HEAD_DIM ** -0.5

LANES = 128
BF16_ROWS = 16
VMEM_LIMIT_BYTES = 56 * 1024 * 1024

TOKEN_TILE = 512
PROJ_ROW_CHUNKS = 1
OUTPROJ_ROW_CHUNKS = 2
ATTN_TQ = 512
ATTN_TK = 512
ATTN_A_TILES_PER_BODY = 8
ATTN_B_TILES_PER_BODY = 4
SUB = 128
GATHER_WIN = SUB + 16
GATHER_WIN_SPARSE = 64
GATHER_TB = 2048
GATHER_EXPERTS_PER_STEP = 2
COMBINE_TB = 512
COMBINE_WIN = 256
COMBINE_ROWS_SPARSE = 144
FFN_TF = 256
FFN_ROWS = 512
NEG_BIG = -1e30

SAFE_RANGE = 40.0
NORM_SLACK = 1.0 + 2.0 ** -7


def _dot(a, b):
    return jnp.dot(a, b, preferred_element_type=F32)


def _cparams(sem):
    return pltpu.CompilerParams(dimension_semantics=sem, vmem_limit_bytes=VMEM_LIMIT_BYTES)


def _split_dot(v, w_bf16):
    hi = v.astype(BF16)
    lo = (v - hi.astype(F32)).astype(BF16)
    return _dot(hi, w_bf16) + _dot(lo, w_bf16)


def _proj_kernel(x_ref, gmix_ref, w_ref, gq_ref, gk_ref, cos_ref, sin_ref, seg_ref,
                 qa_ref, ka_ref, vat_ref, qb_ref, kb_ref, vbt_ref, st_ref):
    seg = seg_ref[...]
    lane = lax.broadcasted_iota(I32, (1, LANES), 1)
    first_half = (lane % 32) < 16
    low_head = lane < HEAD_DIM
    ones_lane64 = jnp.where(lane == HEAD_DIM, 1.0, 0.0)

    def max_head_sumsq(vals):
        sq = vals.astype(F32)
        return jnp.max(_dot((sq * sq).astype(BF16), seg), axis=0, keepdims=True)

    def key_rows(k_pair, odd):
        kh = pltpu.roll(k_pair, HEAD_DIM, 1) if odd else k_pair
        return jnp.where(low_head, kh, ones_lane64).astype(BF16)

    def chunk(rows):
        x = x_ref[rows, :]
        h = (x * lax.rsqrt(jnp.mean(x * x, axis=-1, keepdims=True) + NORM_EPS) * gmix_ref[...]).astype(BF16)
        cos = cos_ref[rows, :]
        sin = sin_ref[rows, :]
        ones_row = jnp.where(lax.broadcasted_iota(I32, (BF16_ROWS, x.shape[0]), 0) == 0, 1.0, 0.0)

        def norm_rope(y, gain):
            ss = _split_dot(y * y, seg)
            y = y * lax.rsqrt(ss * (1.0 / HEAD_DIM) + NORM_EPS) * gain
            rot = jnp.where(first_half, pltpu.roll(y, LANES - 16, 1), pltpu.roll(y, 16, 1))
            return y * cos + rot * sin

        def value_rows_t(vt_rows):
            return jnp.concatenate([vt_rows, ones_row], axis=0).astype(BF16)

        c0 = 0
        aq = _dot(h, w_ref[:, c0:c0 + A_Q_COLS])
        qa_stat = jnp.zeros((1, LANES), F32)
        for c in range(A_Q_COLS // LANES):
            y = (norm_rope(aq[:, c * LANES:(c + 1) * LANES], gq_ref[...]) * QK_SCALE).astype(BF16)
            qa_ref[rows, c * LANES:(c + 1) * LANES] = y
            qa_stat = jnp.maximum(qa_stat, max_head_sumsq(y))
        c0 += A_Q_COLS

        akv = _dot(h, w_ref[:, c0:c0 + 2 * A_KV_COLS])
        k = norm_rope(akv[:, :LANES], gk_ref[...]).astype(BF16)
        ka_stat = max_head_sumsq(k)
        k = k.astype(F32)
        vt = akv[:, LANES:].T
        for g in range(A_KV_HEADS):
            ka_ref[g, rows, :] = key_rows(k, g)
            vat_ref[g, :, rows] = value_rows_t(vt[g * HEAD_DIM:(g + 1) * HEAD_DIM])
        c0 += 2 * A_KV_COLS

        qb = (_dot(h, w_ref[:, c0:c0 + B_QK_COLS]) * QK_SCALE).astype(BF16)
        qb_ref[rows, :] = qb
        c0 += B_QK_COLS
        bk = _dot(h, w_ref[:, c0:c0 + B_QK_COLS]).astype(BF16)
        c0 += B_QK_COLS
        bv = _dot(h, w_ref[:, c0:c0 + B_V_COLS])
        qb_stat = jnp.zeros((1, LANES), F32)
        kb_stat = jnp.zeros((1, LANES), F32)
        for hd in range(B_HEADS):
            blk = slice(hd * LANES, (hd + 1) * LANES)
            qb_stat = jnp.maximum(qb_stat, max_head_sumsq(qb[:, blk]))
            kb_stat = jnp.maximum(kb_stat, max_head_sumsq(bk[:, blk]))
            k_pair = bk[:, blk].astype(F32)
            kb_ref[hd, 0, rows, :] = key_rows(k_pair, 0)
            kb_ref[hd, 1, rows, :] = key_rows(k_pair, 1)
            vbt_ref[hd, :, rows] = value_rows_t(bv[:, blk].T)
        return qa_stat, ka_stat, qb_stat, kb_stat

    per_chunk = [chunk(rows) for rows in _row_chunks(x_ref.shape[0], PROJ_ROW_CHUNKS)]
    stats = [functools.reduce(jnp.maximum, s) for s in zip(*per_chunk)]
    stats = [jnp.broadcast_to(jnp.max(s, axis=1, keepdims=True), (1, LANES)) for s in stats]
    st_ref[0] = jnp.concatenate(stats + [jnp.zeros((8 - len(stats), LANES), F32)], axis=0)


def _proj(x, gmix, w_in, gq, gk, cos, sin, seg, n):
    ntok, d = x.shape
    tm = min(TOKEN_TILE, n)
    per_seq = n // tm
    ntiles = ntok // tm
    va_rows = HEAD_DIM + BF16_ROWS
    vb_rows = B_V_DIM + BF16_ROWS
    tok = lambda i: (i, 0)
    full = lambda i: (0, 0)
    return pl.pallas_call(
        _proj_kernel,
        grid=(ntiles,),
        in_specs=[
            pl.BlockSpec((tm, d), tok),
            pl.BlockSpec((1, d), full),
            pl.BlockSpec(w_in.shape, full),
            pl.BlockSpec((1, LANES), full),
            pl.BlockSpec((1, LANES), full),
            pl.BlockSpec((tm, LANES), lambda i: (i % per_seq, 0)),
            pl.BlockSpec((tm, LANES), lambda i: (i % per_seq, 0)),
            pl.BlockSpec((LANES, LANES), full),
        ],
        out_specs=[
            pl.BlockSpec((tm, A_Q_COLS), tok),
            pl.BlockSpec((A_KV_HEADS, tm, LANES), lambda i: (0, i, 0)),
            pl.BlockSpec((A_KV_HEADS, va_rows, tm), lambda i: (0, 0, i)),
            pl.BlockSpec((tm, B_QK_COLS), tok),
            pl.BlockSpec((B_HEADS, 2, tm, LANES), lambda i: (0, 0, i, 0)),
            pl.BlockSpec((B_HEADS, vb_rows, tm), lambda i: (0, 0, i)),
            pl.BlockSpec((1, 8, LANES), lambda i: (i, 0, 0)),
        ],
        out_shape=[
            jax.ShapeDtypeStruct((ntok, A_Q_COLS), BF16),
            jax.ShapeDtypeStruct((A_KV_HEADS, ntok, LANES), BF16),
            jax.ShapeDtypeStruct((A_KV_HEADS, va_rows, ntok), BF16),
            jax.ShapeDtypeStruct((ntok, B_QK_COLS), BF16),
            jax.ShapeDtypeStruct((B_HEADS, 2, ntok, LANES), BF16),
            jax.ShapeDtypeStruct((B_HEADS, vb_rows, ntok), BF16),
            jax.ShapeDtypeStruct((ntiles, 8, LANES), F32),
        ],
        compiler_params=_cparams(("arbitrary",)),
        name="proj",
    )(x, gmix, w_in, gq, gk, cos, sin, seg)


def _query_operands(q_pair, kmax, fixed_shift):
    qt = q_pair.astype(F32).T
    tq = qt.shape[1]
    first_row = lax.broadcasted_iota(I32, (HEAD_DIM, tq), 0) == 0
    out = []
    for half in (qt[:HEAD_DIM], qt[HEAD_DIM:]):
        tail = jnp.zeros((HEAD_DIM, tq), F32)
        if fixed_shift:
            shift = jnp.sqrt(jnp.sum(half * half, axis=0, keepdims=True)) * kmax
            tail = jnp.where(first_row, -shift, tail)
        out.append(jnp.concatenate([half, tail], axis=0).astype(BF16))
    return out


def _online_step(k, qt, vt, state, bias=None):
    m, acc = state
    s = _dot(k, qt)
    if bias is not None:
        s = s + bias
    m_new = jnp.maximum(m, jnp.max(s, axis=0, keepdims=True))
    acc = jnp.exp(m - m_new) * acc + _dot(vt, jnp.exp(s - m_new).astype(BF16))
    return m_new, acc


def _tiles_per_body(nk, preferred):
    per_body = preferred
    while nk % per_body:
        per_body //= 2
    assert per_body >= 2 and per_body % 2 == 0, "the two-slot score pipeline walks key tiles in pairs"
    return per_body


def _two_slot_pipeline(nk, s_ref, score_fn, weigh_fn, accs, per_body):
    n_maps = s_ref.shape[1]

    def put(slot, j):
        for h, s in enumerate(score_fn(j)):
            s_ref[slot, h] = s

    def take(slot, j, accs):
        return weigh_fn(j, tuple(s_ref[slot, h] for h in range(n_maps)), accs)

    def run(first, count, accs):
        for t in range(count):
            put((t + 1) % 2, first + t + 1)
            accs = take(t % 2, first + t, accs)
        return accs

    put(0, 0)
    accs = lax.fori_loop(0, nk // per_body - 1,
                         lambda i, accs: run(per_body * i, per_body, accs), accs)
    accs = run(nk - per_body, per_body - 1, accs)
    return take((per_body - 1) % 2, nk - 1, accs)


def _score_scratch(fixed_shift, n_maps, tq, tk):
    return [pltpu.VMEM((2, n_maps, tk, tq), F32)] if fixed_shift else []


def _attn_a_kernel(kmax_ref, q_ref, k_ref, vt_ref, o_ref, *scratch, tk, fixed_shift):
    tq = q_ref.shape[0]
    nk = k_ref.shape[1] // tk
    n_pairs = q_ref.shape[1] // LANES
    kmax = kmax_ref[:, :1]
    qts = []
    for pr in range(n_pairs):
        qts += _query_operands(q_ref[:, pr * LANES:(pr + 1) * LANES], kmax, fixed_shift)

    def tiles(j):
        off = pl.multiple_of(j * tk, tk)
        return k_ref[0, pl.ds(off, tk), :], vt_ref[0, :, pl.ds(off, tk)]

    zero = jnp.zeros((vt_ref.shape[1], tq), F32)
    if fixed_shift:
        def scores(j):
            k = tiles(j)[0]
            return tuple(_dot(k, qt) for qt in qts)

        def weighted(j, ss, accs):
            vt = tiles(j)[1]
            return tuple(acc + _dot(vt, jnp.exp(s).astype(BF16)) for s, acc in zip(ss, accs))

        accs = _two_slot_pipeline(nk, scratch[0], scores, weighted, tuple(zero for _ in qts),
                                  _tiles_per_body(nk, ATTN_A_TILES_PER_BODY))
    else:
        def body(j, states):
            k, vt = tiles(j)
            return tuple(_online_step(k, qt, vt, st) for qt, st in zip(qts, states))
        init = (jnp.full((1, tq), NEG_BIG, F32), zero)
        accs = [st[1] for st in lax.fori_loop(0, nk, body, tuple(init for _ in qts))]

    for pr in range(n_pairs):
        halves = [acc[:HEAD_DIM] / acc[HEAD_DIM:HEAD_DIM + 1] for acc in accs[2 * pr:2 * pr + 2]]
        o_ref[:, pr * LANES:(pr + 1) * LANES] = jnp.concatenate(halves, axis=0).T.astype(BF16)


def _attn_a(kmax, qa, ka, vat, *, bsz, n, fixed_shift):
    ntok = qa.shape[0]
    tq = min(ATTN_TQ, n)
    tk = min(ATTN_TK, n)
    nq = n // tq
    gcols = A_Q_COLS // A_KV_HEADS
    return pl.pallas_call(
        functools.partial(_attn_a_kernel, tk=tk, fixed_shift=fixed_shift),
        grid=(bsz, A_KV_HEADS, nq),
        in_specs=[
            pl.BlockSpec((1, LANES), lambda b, g, i: (0, 0)),
            pl.BlockSpec((tq, gcols), lambda b, g, i: (b * nq + i, g)),
            pl.BlockSpec((1, n, LANES), lambda b, g, i: (g, b, 0)),
            pl.BlockSpec((1, vat.shape[1], n), lambda b, g, i: (g, 0, b)),
        ],
        out_specs=pl.BlockSpec((tq, gcols), lambda b, g, i: (b * nq + i, g)),
        out_shape=jax.ShapeDtypeStruct((ntok, A_Q_COLS), BF16),
        scratch_shapes=_score_scratch(fixed_shift, A_Q_HEADS // A_KV_HEADS, tq, tk),
        compiler_params=_cparams(("arbitrary", "arbitrary", "arbitrary")),
        name="attn_a_shift" if fixed_shift else "attn_a_online",
    )(kmax, qa, ka, vat)


def _attn_b_kernel(kmax_ref, q_ref, k_ref, vt_ref, diag_ref, lam_ref, gsub_ref, o_ref, bias_ref, *scratch,
                   tk, d0, dstep, fixed_shift):
    tq = q_ref.shape[0]
    nk = k_ref.shape[2] // tk
    n_tiles = bias_ref.shape[0]
    qi = pl.program_id(2)
    qts = _query_operands(q_ref[...], kmax_ref[:, :1], fixed_shift)

    @pl.when(qi == 0)
    def _():
        for t in range(n_tiles):
            by_offset = jnp.broadcast_to(diag_ref[0, t], (tk, diag_ref.shape[-1]))
            bias_ref[t] = pltpu.roll(by_offset, 0, 1, stride=1, stride_axis=0)[:, :tq]

    def tiles(j):
        off = pl.multiple_of(j * tk, tk)
        return (k_ref[0, 0, pl.ds(off, tk), :], k_ref[0, 1, pl.ds(off, tk), :]), vt_ref[0, :, pl.ds(off, tk)]

    def bias_tile(j):
        u = jnp.clip(j * tk - qi * tq - d0 + dstep, 0, (n_tiles - 1) * dstep)
        return bias_ref[lax.div(u, dstep)]

    zero = jnp.zeros((vt_ref.shape[1], tq), F32)
    if fixed_shift:
        def scores(j):
            return tuple(_dot(k, qt) for k, qt in zip(tiles(j)[0], qts))

        def weighted(j, ss, accs):
            vt = tiles(j)[1][:B_V_DIM]
            bias = bias_tile(j)
            new = []
            for s, (acc, den) in zip(ss, accs):
                p = jnp.exp(s + bias)
                new.append((acc + _dot(vt, p.astype(BF16)), den + jnp.sum(p, axis=0, keepdims=True)))
            return tuple(new)

        init = (jnp.zeros((B_V_DIM, tq), F32), jnp.zeros((1, tq), F32))
        accs = _two_slot_pipeline(nk, scratch[0], scores, weighted, (init, init),
                                  _tiles_per_body(nk, ATTN_B_TILES_PER_BODY))
        o1, o2 = [acc / den for acc, den in accs]
    else:
        def body(j, states):
            ks, vt = tiles(j)
            bias = bias_tile(j)
            return tuple(_online_step(k, qt, vt, st, bias) for k, qt, st in zip(ks, qts, states))
        init = (jnp.full((1, tq), NEG_BIG, F32), zero)
        accs = [st[1] for st in lax.fori_loop(0, nk, body, (init, init))]
        o1, o2 = [acc[:B_V_DIM] / acc[B_V_DIM:B_V_DIM + 1] for acc in accs]
    lp = lam_ref[...]
    lam = (jnp.exp(jnp.sum(lp[0:1] * lp[1:2], axis=-1, keepdims=True))
           - jnp.exp(jnp.sum(lp[2:3] * lp[3:4], axis=-1, keepdims=True)) + LAMBDA_INIT)
    out = (o1 - lam * o2).T
    out = out * lax.rsqrt(jnp.mean(out * out, axis=-1, keepdims=True) + NORM_EPS) * gsub_ref[...]
    o_ref[...] = (out * (1.0 - LAMBDA_INIT)).astype(BF16)


def _t5_bucket(rel):
    nb = REL_BUCKETS // 2
    ret = (rel > 0).astype(I32) * nb
    n = jnp.abs(rel)
    max_exact = nb // 2
    nf = jnp.maximum(n, 1).astype(F32)
    large = max_exact + (jnp.log(nf / max_exact) / math.log(REL_MAX_DIST / max_exact)
                         * (nb - max_exact)).astype(I32)
    large = jnp.minimum(large, nb - 1)
    return ret + jnp.where(n < max_exact, n, large)


def _bias_diagonals(rel_bias, tq, tk):
    dstep = math.gcd(tq, tk)
    reach = (tq + tk + REL_MAX_DIST) // dstep
    band = [d for d in range(-reach * dstep, reach * dstep + 1, dstep)
            if d + tk - 1 > -REL_MAX_DIST and d - (tq - 1) < REL_MAX_DIST]
    far = tq + tk + REL_MAX_DIST
    ds = jnp.asarray([-far] + band + [far], I32)
    span = tq + tk
    m = jnp.arange(span, dtype=I32)
    q_minus_k = jnp.where(m < tq, m, m - span)
    bucket = _t5_bucket(ds[:, None] - q_minus_k[None, :])
    table = rel_bias.astype(F32)
    table = table - jnp.max(table, axis=0, keepdims=True)
    diag = jnp.moveaxis(table[bucket], -1, 0)
    return diag[:, :, None, :], band[0], dstep


def _attn_b(kmax, qb, kb, vbt, diag, lam_params, gsub, *, bsz, n, tq, tk, d0, dstep, fixed_shift):
    ntok = qb.shape[0]
    nq = n // tq
    return pl.pallas_call(
        functools.partial(_attn_b_kernel, tk=tk, d0=d0, dstep=dstep, fixed_shift=fixed_shift),
        grid=(bsz, B_HEADS, nq),
        in_specs=[
            pl.BlockSpec((1, LANES), lambda b, h, i: (0, 0)),
            pl.BlockSpec((tq, LANES), lambda b, h, i: (b * nq + i, h)),
            pl.BlockSpec((1, 2, n, LANES), lambda b, h, i: (h, 0, b, 0)),
            pl.BlockSpec((1, vbt.shape[1], n), lambda b, h, i: (h, 0, b)),
            pl.BlockSpec((1,) + diag.shape[1:], lambda b, h, i: (h, 0, 0, 0)),
            pl.BlockSpec(lam_params.shape, lambda b, h, i: (0, 0)),
            pl.BlockSpec((1, LANES), lambda b, h, i: (0, 0)),
        ],
        out_specs=pl.BlockSpec((tq, LANES), lambda b, h, i: (b * nq + i, h)),
        out_shape=jax.ShapeDtypeStruct((ntok, B_V_COLS), BF16),
        scratch_shapes=[pltpu.VMEM((diag.shape[1], tk, tq), F32)] + _score_scratch(fixed_shift, 2, tq, tk),
        compiler_params=_cparams(("arbitrary", "arbitrary", "arbitrary")),
        name="attn_b_shift" if fixed_shift else "attn_b_online",
    )(kmax, qb, kb, vbt, diag, lam_params, gsub)


def _outproj_kernel(x_ref, a_ref, b_ref, wo_ref, gffn_ref, wr_ref,
                    x1_ref, h2_ref, afft_ref, *, n_experts):
    wa = wo_ref.shape[0] // 2
    lane = lax.broadcasted_iota(I32, (1, LANES), 1)
    for rows in _row_chunks(x_ref.shape[0], OUTPROJ_ROW_CHUNKS):
        x1 = x_ref[rows, :] + _dot(a_ref[rows, :], wo_ref[:wa]) + _dot(b_ref[rows, :], wo_ref[wa:])
        x1_ref[rows, :] = x1
        h2 = x1 * lax.rsqrt(jnp.mean(x1 * x1, axis=-1, keepdims=True) + NORM_EPS) * gffn_ref[...]
        hi = h2.astype(BF16)
        h2_ref[rows, :] = hi
        lo = (h2 - hi.astype(F32)).astype(BF16)
        both = _dot(hi, wr_ref[...])
        logits = both[:, :LANES] + both[:, LANES:] + _dot(lo, wr_ref[:, :LANES])
        logits = jnp.where(lane < n_experts, logits, NEG_BIG)
        ex = jnp.exp(logits - jnp.max(logits, axis=-1, keepdims=True))
        aff = ex / jnp.sum(ex, axis=-1, keepdims=True)
        afft_ref[:, rows] = aff.T[:n_experts]


def _row_chunks(n_rows, n_chunks):
    chunk = max(n_rows // n_chunks, LANES)
    return [slice(r, r + chunk) for r in range(0, n_rows, chunk)]


def _outproj(x, a_out, b_out, w_out, gffn, wr, n_experts):
    ntok, d = x.shape
    tm = min(TOKEN_TILE, ntok)
    tok = lambda i: (i, 0)
    full = lambda i: (0, 0)
    return pl.pallas_call(
        functools.partial(_outproj_kernel, n_experts=n_experts),
        grid=(ntok // tm,),
        in_specs=[
            pl.BlockSpec((tm, d), tok),
            pl.BlockSpec((tm, a_out.shape[1]), tok),
            pl.BlockSpec((tm, b_out.shape[1]), tok),
            pl.BlockSpec(w_out.shape, full),
            pl.BlockSpec((1, d), full),
            pl.BlockSpec(wr.shape, full),
        ],
        out_specs=[
            pl.BlockSpec((tm, d), tok),
            pl.BlockSpec((tm, d), tok),
            pl.BlockSpec((n_experts, tm), lambda i: (0, i)),
        ],
        out_shape=[
            jax.ShapeDtypeStruct((ntok, d), F32),
            jax.ShapeDtypeStruct((ntok, d), BF16),
            jax.ShapeDtypeStruct((n_experts, ntok), F32),
        ],
        compiler_params=_cparams(("arbitrary",)),
        name="outproj",
    )(x, a_out, b_out, w_out, gffn, wr)


def _topk_kernel(aff_ref, slot_ref, gsel_ref, cstart_ref, *, cap):
    aff = aff_ref[...]
    n_exp, ntok = aff.shape

    def count(mask):
        return jnp.sum(mask.astype(I32), axis=1, keepdims=True)

    def as_float(bits):
        return lax.bitcast_convert_type(bits, np.float32)

    def value_bit(i, t):
        cand = t | lax.shift_left(jnp.int32(1), 30 - i)
        return jnp.where(count(aff >= as_float(cand)) >= cap, cand, t)

    thr = lax.fori_loop(0, 31, value_bit, jnp.zeros((n_exp, 1), I32))
    above = aff >= as_float(thr + 1)
    tie = (aff >= as_float(thr)) & jnp.logical_not(above)
    need = cap - count(above)
    idx = lax.broadcasted_iota(I32, aff.shape, 1)

    n_bits = int(ntok).bit_length()

    def index_bit(i, bound):
        cand = bound + lax.shift_left(jnp.int32(1), n_bits - 1 - i)
        ok = (cand <= ntok) & (count(tie & (idx < cand)) <= need)
        return jnp.where(ok, cand, bound)

    bound = lax.fori_loop(0, n_bits, index_bit, jnp.zeros((n_exp, 1), I32))
    sel = above | (tie & (idx < bound))
    gsel_ref[...] = jnp.where(sel, aff, 0.0)

    r = lax.broadcasted_iota(I32, (SUB, SUB), 0)
    c = lax.broadcasted_iota(I32, (SUB, SUB), 1)
    before = jnp.where(r < c, 1.0, 0.0).astype(BF16)
    carry = jnp.zeros((n_exp, 1), F32)
    n_sub = ntok // SUB
    for s in range(n_sub):
        sc = sel[:, s * SUB:(s + 1) * SUB]
        scf = jnp.where(sc, 1.0, 0.0)
        pos = (_dot(scf.astype(BF16), before) + carry).astype(I32)
        slot_ref[:, s * SUB:(s + 1) * SUB] = jnp.where(sc, pos, -1)
        cstart_ref[:, s:s + 1] = carry.astype(I32)
        carry = carry + jnp.sum(scf, axis=1, keepdims=True)


def _topk(afft, cap):
    n_exp, ntok = afft.shape
    nsub = ntok // SUB
    return pl.pallas_call(
        functools.partial(_topk_kernel, cap=cap),
        out_shape=[
            jax.ShapeDtypeStruct((n_exp, ntok), I32),
            jax.ShapeDtypeStruct((n_exp, ntok), F32),
            jax.ShapeDtypeStruct((n_exp, nsub), I32),
        ],
        compiler_params=pltpu.CompilerParams(vmem_limit_bytes=VMEM_LIMIT_BYTES),
        name="topk",
    )(afft)


def _gather_kernel(cstart_ref, dense_ref, slot_ref, gsel_ref, h_ref, xe_ref, gs_ref):
    i = pl.program_id(1)
    n_sub = h_ref.shape[0] // SUB
    per_step = xe_ref.shape[0]

    @pl.when(i == 0)
    def _():
        xe_ref[...] = jnp.zeros_like(xe_ref)
        gs_ref[...] = jnp.zeros_like(gs_ref)

    def compact(x, win):
        e = pl.program_id(0) * per_step + x
        row = lax.broadcasted_iota(I32, (win, SUB), 0)
        for s in range(n_sub):
            cs = cstart_ref[e, i * n_sub + s]
            rows = pl.ds(pl.multiple_of(lax.div(cs, 16) * 16, 16), win)
            sl = slot_ref[x, :, s * SUB:(s + 1) * SUB]
            hit = row == (sl - rows.start)
            onehot = jnp.where(hit, 1.0, 0.0).astype(BF16)
            moved = _dot(onehot, h_ref[s * SUB:(s + 1) * SUB, :])
            xe_ref[x, rows, :] = (xe_ref[x, rows, :].astype(F32) + moved).astype(BF16)
            g = gsel_ref[x, :, s * SUB:(s + 1) * SUB]
            gw = jnp.sum(jnp.where(hit, g, 0.0), axis=1, keepdims=True)
            gs_ref[x, rows, :] += jnp.broadcast_to(gw, (win, LANES))

    for x in range(per_step):
        dense = dense_ref[pl.program_id(0) * per_step + x, i]
        pl.when(dense == 0)(functools.partial(compact, x, GATHER_WIN_SPARSE))
        pl.when(dense != 0)(functools.partial(compact, x, GATHER_WIN))


def _gather(cstart, slot3, gsel3, h2, cap):
    n_exp = slot3.shape[0]
    ntok, d = h2.shape
    tb = min(GATHER_TB, ntok)
    ends = jnp.concatenate([cstart[:, 1:], jnp.full((n_exp, 1), cap, I32)], axis=1)
    most = jnp.max((ends - cstart).reshape(n_exp, ntok // tb, tb // SUB), axis=-1)
    dense = (most > GATHER_WIN_SPARSE - 16).astype(I32)
    per_step = math.gcd(GATHER_EXPERTS_PER_STEP, n_exp)
    rows = cap + GATHER_WIN
    return pl.pallas_call(
        _gather_kernel,
        grid_spec=pltpu.PrefetchScalarGridSpec(
            num_scalar_prefetch=2,
            grid=(n_exp // per_step, ntok // tb),
            in_specs=[
                pl.BlockSpec((per_step, 1, tb), lambda e, i, cs, dn: (e, 0, i)),
                pl.BlockSpec((per_step, 1, tb), lambda e, i, cs, dn: (e, 0, i)),
                pl.BlockSpec((tb, d), lambda e, i, cs, dn: (i, 0)),
            ],
            out_specs=[
                pl.BlockSpec((per_step, rows, d), lambda e, i, cs, dn: (e, 0, 0)),
                pl.BlockSpec((per_step, rows, LANES), lambda e, i, cs, dn: (e, 0, 0)),
            ],
        ),
        out_shape=[
            jax.ShapeDtypeStruct((n_exp, rows, d), BF16),
            jax.ShapeDtypeStruct((n_exp, rows, LANES), F32),
        ],
        compiler_params=_cparams(("arbitrary", "arbitrary")),
        name="gather",
    )(cstart, dense, slot3, gsel3, h2)


def _ffn_kernel(xe_ref, gs_ref, wg_ref, wu_ref, wd_ref, ye_ref, acc_ref):
    f = pl.program_id(1)
    cap = xe_ref.shape[1]
    rows = min(FFN_ROWS, cap)

    @pl.when(f == 0)
    def _():
        acc_ref[...] = jnp.zeros_like(acc_ref)

    wg = wg_ref[0].astype(BF16)
    wu = wu_ref[0].astype(BF16)
    wd = wd_ref[0].astype(BF16)
    for c in range(cap // rows):
        x = xe_ref[0, c * rows:(c + 1) * rows, :]
        g = _dot(x, wg)
        u = _dot(x, wu)
        hid = (g * jax.nn.sigmoid(g) * u).astype(BF16)
        acc_ref[c * rows:(c + 1) * rows, :] += _dot(hid, wd)

    @pl.when(f == pl.num_programs(1) - 1)
    def _():
        ye_ref[0] = (acc_ref[...] * gs_ref[0][:, :1]).astype(BF16)


def _ffn(xe, gs, w_gate, w_up, w_down, cap):
    n_exp, _, d = xe.shape
    ff = w_gate.shape[-1]
    tf = min(FFN_TF, ff)
    return pl.pallas_call(
        _ffn_kernel,
        grid=(n_exp, ff // tf),
        in_specs=[
            pl.BlockSpec((1, cap, d), lambda e, f: (e, 0, 0)),
            pl.BlockSpec((1, cap, LANES), lambda e, f: (e, 0, 0)),
            pl.BlockSpec((1, d, tf), lambda e, f: (e, 0, f)),
            pl.BlockSpec((1, d, tf), lambda e, f: (e, 0, f)),
            pl.BlockSpec((1, tf, d), lambda e, f: (e, f, 0)),
        ],
        out_specs=pl.BlockSpec((1, cap, d), lambda e, f: (e, 0, 0)),
        out_shape=jax.ShapeDtypeStruct((n_exp, cap, d), BF16),
        scratch_shapes=[pltpu.VMEM((cap, d), F32)],
        compiler_params=_cparams(("arbitrary", "arbitrary")),
        name="ffn",
    )(xe, gs, w_gate, w_up, w_down)


def _window_start(first_slot, cap, rows):
    return jnp.minimum(lax.div(first_slot, 16), (cap - rows) // 16) * 16


def _combine_kernel(cstart_ref, x1_ref, slot_ref, *refs, cap):
    *ye_refs, gfin_ref, y_ref = refs
    i = pl.program_id(0)
    n_sub = x1_ref.shape[0] // SUB
    rows = ye_refs[0].shape[1]
    win = min(COMBINE_WIN, rows)
    row = lax.broadcasted_iota(I32, (win, SUB), 0)
    dims = (((0,), (0,)), ((), ()))
    starts = [_window_start(cstart_ref[e, i * n_sub], cap, rows) for e in range(len(ye_refs))]
    for s in range(n_sub):
        out = x1_ref[s * SUB:(s + 1) * SUB, :]
        for e, ye_ref in enumerate(ye_refs):
            cs16 = lax.div(cstart_ref[e, i * n_sub + s], 16) * 16
            w0 = pl.multiple_of(jnp.clip(cs16 - starts[e], 0, rows - win), 16)
            sl = slot_ref[e, :, s * SUB:(s + 1) * SUB]
            onehot = jnp.where(row == (sl - (starts[e] + w0)), 1.0, 0.0).astype(BF16)
            out = out + lax.dot_general(onehot, ye_ref[0, pl.ds(w0, win), :], dims,
                                        preferred_element_type=F32)
        y_ref[s * SUB:(s + 1) * SUB, :] = (
            out * lax.rsqrt(jnp.mean(out * out, axis=-1, keepdims=True) + NORM_EPS) * gfin_ref[...])


def _combine_call(cstart, x1, slot3, ye, gfin, *, tb, rows):
    n_exp, cap, d = ye.shape
    ntok = x1.shape[0]
    n_sub = tb // SUB

    def window(e):
        return pl.BlockSpec((pl.Element(1), pl.Element(rows), pl.Element(d)),
                            lambda i, cs: (e, _window_start(cs[e, i * n_sub], cap, rows), 0))

    return pl.pallas_call(
        functools.partial(_combine_kernel, cap=cap),
        grid_spec=pltpu.PrefetchScalarGridSpec(
            num_scalar_prefetch=1,
            grid=(ntok // tb,),
            in_specs=[pl.BlockSpec((tb, d), lambda i, cs: (i, 0)),
                      pl.BlockSpec((n_exp, 1, tb), lambda i, cs: (0, 0, i))]
                     + [window(e) for e in range(n_exp)]
                     + [pl.BlockSpec((1, d), lambda i, cs: (0, 0))],
            out_specs=pl.BlockSpec((tb, d), lambda i, cs: (i, 0)),
        ),
        out_shape=jax.ShapeDtypeStruct((ntok, d), F32),
        compiler_params=_cparams(("arbitrary",)),
        name="combine",
    )(cstart, x1, slot3, *([ye] * n_exp), gfin)


def _combine(cstart, x1, slot3, ye, gfin):
    n_exp, cap, d = ye.shape
    ntok = x1.shape[0]
    tb = min(COMBINE_TB, ntok, cap // 2)
    n_sub = tb // SUB
    rows_dense = tb + 16
    rows_sparse = min(COMBINE_ROWS_SPARSE, rows_dense)
    assert rows_sparse >= SUB + 16 and rows_dense <= cap
    first = cstart[:, ::n_sub]
    used = jnp.concatenate([first[:, 1:], jnp.full((n_exp, 1), cap, I32)], axis=1) - first
    call = functools.partial(_combine_call, tb=tb)
    return lax.cond(jnp.max(used) <= rows_sparse - 16,
                    functools.partial(call, rows=rows_sparse),
                    functools.partial(call, rows=rows_dense),
                    cstart, x1, slot3, ye, gfin)


def _rope_tables(n):
    rows = n // GRID_W
    row = jnp.broadcast_to(jnp.arange(rows)[:, None], (rows, GRID_W)).reshape(-1).astype(F32)
    col = jnp.broadcast_to(jnp.arange(GRID_W)[None, :], (rows, GRID_W)).reshape(-1).astype(F32)
    half = HEAD_DIM // 2
    inv = ROPE_THETA ** (-jnp.arange(0, half, 2, dtype=F32) / half)
    ang_r = row[:, None] * inv[None, :]
    ang_c = col[:, None] * inv[None, :]
    cos = jnp.concatenate([jnp.cos(ang_r), jnp.cos(ang_r), jnp.cos(ang_c), jnp.cos(ang_c)], axis=-1)
    sin = jnp.concatenate([-jnp.sin(ang_r), jnp.sin(ang_r), -jnp.sin(ang_c), jnp.sin(ang_c)], axis=-1)
    return jnp.tile(cos, (1, 2)), jnp.tile(sin, (1, 2))


def _encoder(x, p):
    bsz, n, d = x.shape
    ntok = bsz * n
    n_exp = p["n_exp"]
    cap = CAPACITY_FACTOR * ntok // n_exp
    xt = x.reshape(ntok, d)
    cos, sin = _rope_tables(n)
    qa, ka, vat, qb, kb, vbt, stats = _proj(xt, p["gmix"], p["w_in"], p["gq"], p["gk"], cos, sin,
                                            p["seg"], n)

    qa_n, ka_n, qb_n, kb_n = jnp.sqrt(jnp.max(stats[:, :4, 0], axis=0)) * NORM_SLACK
    lanes = lambda v: jnp.broadcast_to(v, (1, LANES)).astype(F32)
    a_args = (lanes(ka_n), qa, ka, vat)
    a_out = lax.cond(qa_n * ka_n <= SAFE_RANGE,
                     functools.partial(_attn_a, bsz=bsz, n=n, fixed_shift=True),
                     functools.partial(_attn_a, bsz=bsz, n=n, fixed_shift=False), *a_args)
    tq, tk = min(ATTN_TQ, n), min(ATTN_TK, n)
    diag, d0, dstep = _bias_diagonals(p["rel_bias"], tq, tk)
    b_args = (lanes(kb_n), qb, kb, vbt, diag, p["lam"], p["gsub"])
    b_kw = dict(bsz=bsz, n=n, tq=tq, tk=tk, d0=d0, dstep=dstep)
    b_out = lax.cond(2.0 * qb_n * kb_n + p["bias_spread"] <= 2.0 * SAFE_RANGE,
                     functools.partial(_attn_b, fixed_shift=True, **b_kw),
                     functools.partial(_attn_b, fixed_shift=False, **b_kw), *b_args)

    x1, h2, afft = _outproj(xt, a_out, b_out, p["w_out"], p["gffn"], p["wr"], n_exp)
    slot, gsel, cstart = _topk(afft, cap)
    slot3 = slot.reshape(n_exp, 1, ntok)
    gsel3 = gsel.reshape(n_exp, 1, ntok)
    xe, gs = _gather(cstart, slot3, gsel3, h2, cap)
    ye = _ffn(xe, gs, p["w_gate"], p["w_up"], p["w_down"], cap)
    y = _combine(cstart, x1, slot3, ye, p["gfin"])
    return y.reshape(bsz, n, d)


def kernel(x_prompt, x_sample, norm_mix, w_in, a_q_norm, a_k_norm, b_lambda, b_subln, w_out, rel_bias,
           norm_ffn, w_router, w_gate, w_up, w_down, norm_final):
    d = x_prompt.shape[-1]
    n_exp = w_router.shape[-1]
    wr = jnp.zeros((d, LANES), F32).at[:, :n_exp].set(w_router[0].astype(F32))
    wr_hi = wr.astype(BF16)
    lane_head = np.arange(LANES) // HEAD_DIM
    p = dict(
        n_exp=n_exp,
        gmix=norm_mix[0].astype(F32).reshape(1, d),
        w_in=w_in[0].astype(BF16),
        gq=jnp.tile(a_q_norm[0].astype(F32), 2).reshape(1, LANES),
        gk=jnp.tile(a_k_norm[0].astype(F32), 2).reshape(1, LANES),
        seg=jnp.asarray(lane_head[:, None] == lane_head[None, :], BF16),
        rel_bias=rel_bias,
        bias_spread=(jnp.max(rel_bias) - jnp.min(rel_bias)).astype(F32),
        lam=b_lambda[0].astype(F32),
        gsub=b_subln[0].astype(F32).reshape(1, LANES),
        w_out=w_out[0].astype(BF16),
        gffn=norm_ffn[0].astype(F32).reshape(1, d),
        wr=jnp.concatenate([wr_hi, (wr - wr_hi.astype(F32)).astype(BF16)], axis=1),
        w_gate=w_gate[0], w_up=w_up[0], w_down=w_down[0],
        gfin=norm_final.astype(F32).reshape(1, d),
    )
    return (_encoder(x_prompt, p), _encoder(x_sample, p))
```

```python
import functools
import math

import jax
import jax.numpy as jnp
import numpy as np
from jax import lax
from jax.experimental import pallas as pl
from jax.experimental.pallas import tpu as pltpu

F32 = jnp.float32
BF16 = jnp.bfloat16
I32 = jnp.int32

HEAD_DIM = 64
A_Q_HEADS = 8
A_KV_HEADS = 2
B_HEADS = 4
A_Q_COLS = A_Q_HEADS * HEAD_DIM
A_KV_COLS = A_KV_HEADS * HEAD_DIM
B_QK_COLS = B_HEADS * 2 * HEAD_DIM
B_V_DIM = 2 * HEAD_DIM
B_V_COLS = B_HEADS * B_V_DIM
GRID_W = 64
ROPE_THETA = 10000.0
REL_BUCKETS = 32
REL_MAX_DIST = 128
CAPACITY_FACTOR = 2
NORM_EPS = 1e-6
LAMBDA_INIT = 0.8 - 0.6 * math.exp(-0.3 * 0)
QK_SCALE = HEAD_DIM ** -0.5

LANES = 128
BF16_ROWS = 16
VMEM_LIMIT_BYTES = 56 * 1024 * 1024

TOKEN_TILE = 512
PROJ_ROW_CHUNKS = 1
OUTPROJ_ROW_CHUNKS = 2
ATTN_TQ = 512
ATTN_TK = 512
ATTN_A_TILES_PER_BODY = 8
ATTN_B_TILES_PER_BODY = 8
SUB = 128
GATHER_WIN = SUB + 16
GATHER_WIN_SPARSE = 64
GATHER_TB = 2048
GATHER_EXPERTS_PER_STEP = 2
COMBINE_TB = 512
COMBINE_WIN = 256
COMBINE_ROWS_SPARSE = 144
FFN_TF = 256
FFN_ROWS = 512
NEG_BIG = -1e30

SAFE_RANGE = 40.0
NORM_SLACK = 1.0 + 2.0 ** -7


def _dot(a, b):
    return jnp.dot(a, b, preferred_element_type=F32)


def _cparams(sem):
    return pltpu.CompilerParams(dimension_semantics=sem, vmem_limit_bytes=VMEM_LIMIT_BYTES)


def _split_dot(v, w_bf16):
    hi = v.astype(BF16)
    lo = (v - hi.astype(F32)).astype(BF16)
    return _dot(hi, w_bf16) + _dot(lo, w_bf16)


def _proj_kernel(x_ref, gmix_ref, w_ref, gq_ref, gk_ref, cos_ref, sin_ref, seg_ref,
                 qa_ref, ka_ref, vat_ref, qb_ref, kb_ref, vbt_ref, st_ref):
    seg = seg_ref[...]
    lane = lax.broadcasted_iota(I32, (1, LANES), 1)
    first_half = (lane % 32) < 16
    low_head = lane < HEAD_DIM
    ones_lane64 = jnp.where(lane == HEAD_DIM, 1.0, 0.0)

    def max_head_sumsq(vals):
        sq = vals.astype(F32)
        return jnp.max(_dot((sq * sq).astype(BF16), seg), axis=0, keepdims=True)

    def key_rows(k_pair, odd):
        kh = pltpu.roll(k_pair, HEAD_DIM, 1) if odd else k_pair
        return jnp.where(low_head, kh, ones_lane64).astype(BF16)

    def chunk(rows):
        x = x_ref[rows, :]
        h = (x * lax.rsqrt(jnp.mean(x * x, axis=-1, keepdims=True) + NORM_EPS) * gmix_ref[...]).astype(BF16)
        cos = cos_ref[rows, :]
        sin = sin_ref[rows, :]
        ones_row = jnp.where(lax.broadcasted_iota(I32, (BF16_ROWS, x.shape[0]), 0) == 0, 1.0, 0.0)

        def norm_rope(y, gain):
            ss = _split_dot(y * y, seg)
            y = y * lax.rsqrt(ss * (1.0 / HEAD_DIM) + NORM_EPS) * gain
            rot = jnp.where(first_half, pltpu.roll(y, LANES - 16, 1), pltpu.roll(y, 16, 1))
            return y * cos + rot * sin

        def value_rows_t(vt_rows):
            return jnp.concatenate([vt_rows, ones_row], axis=0).astype(BF16)

        c0 = 0
        aq = _dot(h, w_ref[:, c0:c0 + A_Q_COLS])
        qa_stat = jnp.zeros((1, LANES), F32)
        for c in range(A_Q_COLS // LANES):
            y = (norm_rope(aq[:, c * LANES:(c + 1) * LANES], gq_ref[...]) * QK_SCALE).astype(BF16)
            qa_ref[rows, c * LANES:(c + 1) * LANES] = y
            qa_stat = jnp.maximum(qa_stat, max_head_sumsq(y))
        c0 += A_Q_COLS

        akv = _dot(h, w_ref[:, c0:c0 + 2 * A_KV_COLS])
        k = norm_rope(akv[:, :LANES], gk_ref[...]).astype(BF16)
        ka_stat = max_head_sumsq(k)
        k = k.astype(F32)
        vt = akv[:, LANES:].T
        for g in range(A_KV_HEADS):
            ka_ref[g, rows, :] = key_rows(k, g)
            vat_ref[g, :, rows] = value_rows_t(vt[g * HEAD_DIM:(g + 1) * HEAD_DIM])
        c0 += 2 * A_KV_COLS

        qb = (_dot(h, w_ref[:, c0:c0 + B_QK_COLS]) * QK_SCALE).astype(BF16)
        qb_ref[rows, :] = qb
        c0 += B_QK_COLS
        bk = _dot(h, w_ref[:, c0:c0 + B_QK_COLS]).astype(BF16)
        c0 += B_QK_COLS
        bv = _dot(h, w_ref[:, c0:c0 + B_V_COLS])
        qb_stat = jnp.zeros((1, LANES), F32)
        kb_stat = jnp.zeros((1, LANES), F32)
        for hd in range(B_HEADS):
            blk = slice(hd * LANES, (hd + 1) * LANES)
            qb_stat = jnp.maximum(qb_stat, max_head_sumsq(qb[:, blk]))
            kb_stat = jnp.maximum(kb_stat, max_head_sumsq(bk[:, blk]))
            k_pair = bk[:, blk].astype(F32)
            kb_ref[hd, 0, rows, :] = key_rows(k_pair, 0)
            kb_ref[hd, 1, rows, :] = key_rows(k_pair, 1)
            vbt_ref[hd, :, rows] = value_rows_t(bv[:, blk].T)
        return qa_stat, ka_stat, qb_stat, kb_stat

    per_chunk = [chunk(rows) for rows in _row_chunks(x_ref.shape[0], PROJ_ROW_CHUNKS)]
    stats = [functools.reduce(jnp.maximum, s) for s in zip(*per_chunk)]
    stats = [jnp.broadcast_to(jnp.max(s, axis=1, keepdims=True), (1, LANES)) for s in stats]
    st_ref[0] = jnp.concatenate(stats + [jnp.zeros((8 - len(stats), LANES), F32)], axis=0)


def _proj(x, gmix, w_in, gq, gk, cos, sin, seg, n):
    ntok, d = x.shape
    tm = min(TOKEN_TILE, n)
    per_seq = n // tm
    ntiles = ntok // tm
    va_rows = HEAD_DIM + BF16_ROWS
    vb_rows = B_V_DIM + BF16_ROWS
    tok = lambda i: (i, 0)
    full = lambda i: (0, 0)
    return pl.pallas_call(
        _proj_kernel,
        grid=(ntiles,),
        in_specs=[
            pl.BlockSpec((tm, d), tok),
            pl.BlockSpec((1, d), full),
            pl.BlockSpec(w_in.shape, full),
            pl.BlockSpec((1, LANES), full),
            pl.BlockSpec((1, LANES), full),
            pl.BlockSpec((tm, LANES), lambda i: (i % per_seq, 0)),
            pl.BlockSpec((tm, LANES), lambda i: (i % per_seq, 0)),
            pl.BlockSpec((LANES, LANES), full),
        ],
        out_specs=[
            pl.BlockSpec((tm, A_Q_COLS), tok),
            pl.BlockSpec((A_KV_HEADS, tm, LANES), lambda i: (0, i, 0)),
            pl.BlockSpec((A_KV_HEADS, va_rows, tm), lambda i: (0, 0, i)),
            pl.BlockSpec((tm, B_QK_COLS), tok),
            pl.BlockSpec((B_HEADS, 2, tm, LANES), lambda i: (0, 0, i, 0)),
            pl.BlockSpec((B_HEADS, vb_rows, tm), lambda i: (0, 0, i)),
            pl.BlockSpec((1, 8, LANES), lambda i: (i, 0, 0)),
        ],
        out_shape=[
            jax.ShapeDtypeStruct((ntok, A_Q_COLS), BF16),
            jax.ShapeDtypeStruct((A_KV_HEADS, ntok, LANES), BF16),
            jax.ShapeDtypeStruct((A_KV_HEADS, va_rows, ntok), BF16),
            jax.ShapeDtypeStruct((ntok, B_QK_COLS), BF16),
            jax.ShapeDtypeStruct((B_HEADS, 2, ntok, LANES), BF16),
            jax.ShapeDtypeStruct((B_HEADS, vb_rows, ntok), BF16),
            jax.ShapeDtypeStruct((ntiles, 8, LANES), F32),
        ],
        compiler_params=_cparams(("arbitrary",)),
        name="proj",
    )(x, gmix, w_in, gq, gk, cos, sin, seg)


def _query_operands(q_pair, kmax, fixed_shift):
    qt = q_pair.astype(F32).T
    tq = qt.shape[1]
    first_row = lax.broadcasted_iota(I32, (HEAD_DIM, tq), 0) == 0
    out = []
    for half in (qt[:HEAD_DIM], qt[HEAD_DIM:]):
        tail = jnp.zeros((HEAD_DIM, tq), F32)
        if fixed_shift:
            shift = jnp.sqrt(jnp.sum(half * half, axis=0, keepdims=True)) * kmax
            tail = jnp.where(first_row, -shift, tail)
        out.append(jnp.concatenate([half, tail], axis=0).astype(BF16))
    return out


def _online_step(k, qt, vt, state, bias=None):
    m, acc = state
    s = _dot(k, qt)
    if bias is not None:
        s = s + bias
    m_new = jnp.maximum(m, jnp.max(s, axis=0, keepdims=True))
    acc = jnp.exp(m - m_new) * acc + _dot(vt, jnp.exp(s - m_new).astype(BF16))
    return m_new, acc


def _tiles_per_body(nk, preferred):
    per_body = preferred
    while nk % per_body:
        per_body //= 2
    assert per_body >= 2 and per_body % 2 == 0, "the two-slot score pipeline walks key tiles in pairs"
    return per_body


def _two_slot_pipeline(nk, s_ref, score_fn, weigh_fn, accs, per_body):
    n_maps = s_ref.shape[1]

    def put(slot, j):
        for h, s in enumerate(score_fn(j)):
            s_ref[slot, h] = s

    def take(slot, j, accs):
        return weigh_fn(j, tuple(s_ref[slot, h] for h in range(n_maps)), accs)

    def run(first, count, accs):
        for t in range(count):
            put((t + 1) % 2, first + t + 1)
            accs = take(t % 2, first + t, accs)
        return accs

    put(0, 0)
    accs = lax.fori_loop(0, nk // per_body - 1,
                         lambda i, accs: run(per_body * i, per_body, accs), accs)
    accs = run(nk - per_body, per_body - 1, accs)
    return take((per_body - 1) % 2, nk - 1, accs)


def _score_scratch(fixed_shift, n_maps, tq, tk):
    return [pltpu.VMEM((2, n_maps, tk, tq), F32)] if fixed_shift else []


def _attn_a_kernel(kmax_ref, q_ref, k_ref, vt_ref, o_ref, *scratch, tk, fixed_shift):
    tq = q_ref.shape[0]
    nk = k_ref.shape[1] // tk
    n_pairs = q_ref.shape[1] // LANES
    kmax = kmax_ref[:, :1]
    qts = []
    for pr in range(n_pairs):
        qts += _query_operands(q_ref[:, pr * LANES:(pr + 1) * LANES], kmax, fixed_shift)

    def tiles(j):
        off = pl.multiple_of(j * tk, tk)
        return k_ref[0, pl.ds(off, tk), :], vt_ref[0, :, pl.ds(off, tk)]

    zero = jnp.zeros((vt_ref.shape[1], tq), F32)
    if fixed_shift:
        def scores(j):
            k = tiles(j)[0]
            return tuple(_dot(k, qt) for qt in qts)

        def weighted(j, ss, accs):
            vt = tiles(j)[1]
            return tuple(acc + _dot(vt, jnp.exp(s).astype(BF16)) for s, acc in zip(ss, accs))

        accs = _two_slot_pipeline(nk, scratch[0], scores, weighted, tuple(zero for _ in qts),
                                  _tiles_per_body(nk, ATTN_A_TILES_PER_BODY))
    else:
        def body(j, states):
            k, vt = tiles(j)
            return tuple(_online_step(k, qt, vt, st) for qt, st in zip(qts, states))
        init = (jnp.full((1, tq), NEG_BIG, F32), zero)
        accs = [st[1] for st in lax.fori_loop(0, nk, body, tuple(init for _ in qts))]

    for pr in range(n_pairs):
        halves = [acc[:HEAD_DIM] / acc[HEAD_DIM:HEAD_DIM + 1] for acc in accs[2 * pr:2 * pr + 2]]
        o_ref[:, pr * LANES:(pr + 1) * LANES] = jnp.concatenate(halves, axis=0).T.astype(BF16)


def _attn_a(kmax, qa, ka, vat, *, bsz, n, fixed_shift):
    ntok = qa.shape[0]
    tq = min(ATTN_TQ, n)
    tk = min(ATTN_TK, n)
    nq = n // tq
    gcols = A_Q_COLS // A_KV_HEADS
    return pl.pallas_call(
        functools.partial(_attn_a_kernel, tk=tk, fixed_shift=fixed_shift),
        grid=(bsz, A_KV_HEADS, nq),
        in_specs=[
            pl.BlockSpec((1, LANES), lambda b, g, i: (0, 0)),
            pl.BlockSpec((tq, gcols), lambda b, g, i: (b * nq + i, g)),
            pl.BlockSpec((1, n, LANES), lambda b, g, i: (g, b, 0)),
            pl.BlockSpec((1, vat.shape[1], n), lambda b, g, i: (g, 0, b)),
        ],
        out_specs=pl.BlockSpec((tq, gcols), lambda b, g, i: (b * nq + i, g)),
        out_shape=jax.ShapeDtypeStruct((ntok, A_Q_COLS), BF16),
        scratch_shapes=_score_scratch(fixed_shift, A_Q_HEADS // A_KV_HEADS, tq, tk),
        compiler_params=_cparams(("arbitrary", "arbitrary", "arbitrary")),
        name="attn_a_shift" if fixed_shift else "attn_a_online",
    )(kmax, qa, ka, vat)


def _attn_b_kernel(kmax_ref, q_ref, k_ref, vt_ref, diag_ref, lam_ref, gsub_ref, o_ref, bias_ref, *scratch,
                   tk, d0, dstep, fixed_shift):
    tq = q_ref.shape[0]
    nk = k_ref.shape[2] // tk
    n_tiles = bias_ref.shape[0]
    qi = pl.program_id(2)
    qts = _query_operands(q_ref[...], kmax_ref[:, :1], fixed_shift)

    @pl.when(qi == 0)
    def _():
        for t in range(n_tiles):
            by_offset = jnp.broadcast_to(diag_ref[0, t], (tk, diag_ref.shape[-1]))
            bias_ref[t] = pltpu.roll(by_offset, 0, 1, stride=1, stride_axis=0)[:, :tq]

    def tiles(j):
        off = pl.multiple_of(j * tk, tk)
        return (k_ref[0, 0, pl.ds(off, tk), :], k_ref[0, 1, pl.ds(off, tk), :]), vt_ref[0, :, pl.ds(off, tk)]

    def bias_tile(j):
        u = jnp.clip(j * tk - qi * tq - d0 + dstep, 0, (n_tiles - 1) * dstep)
        return bias_ref[lax.div(u, dstep)]

    zero = jnp.zeros((vt_ref.shape[1], tq), F32)
    if fixed_shift:
        def scores(j):
            return tuple(_dot(k, qt) for k, qt in zip(tiles(j)[0], qts))

        def weighted(j, ss, accs):
            vt = tiles(j)[1]
            bias = bias_tile(j)
            return tuple(acc + _dot(vt, jnp.exp(s + bias).astype(BF16)) for s, acc in zip(ss, accs))

        accs = _two_slot_pipeline(nk, scratch[0], scores, weighted, (zero, zero),
                                  _tiles_per_body(nk, ATTN_B_TILES_PER_BODY))
    else:
        def body(j, states):
            ks, vt = tiles(j)
            bias = bias_tile(j)
            return tuple(_online_step(k, qt, vt, st, bias) for k, qt, st in zip(ks, qts, states))
        init = (jnp.full((1, tq), NEG_BIG, F32), zero)
        accs = [st[1] for st in lax.fori_loop(0, nk, body, (init, init))]

    o1, o2 = [acc[:B_V_DIM] / acc[B_V_DIM:B_V_DIM + 1] for acc in accs]
    lp = lam_ref[...]
    lam = (jnp.exp(jnp.sum(lp[0:1] * lp[1:2], axis=-1, keepdims=True))
           - jnp.exp(jnp.sum(lp[2:3] * lp[3:4], axis=-1, keepdims=True)) + LAMBDA_INIT)
    out = (o1 - lam * o2).T
    out = out * lax.rsqrt(jnp.mean(out * out, axis=-1, keepdims=True) + NORM_EPS) * gsub_ref[...]
    o_ref[...] = (out * (1.0 - LAMBDA_INIT)).astype(BF16)


def _t5_bucket(rel):
    nb = REL_BUCKETS // 2
    ret = (rel > 0).astype(I32) * nb
    n = jnp.abs(rel)
    max_exact = nb // 2
    nf = jnp.maximum(n, 1).astype(F32)
    large = max_exact + (jnp.log(nf / max_exact) / math.log(REL_MAX_DIST / max_exact)
                         * (nb - max_exact)).astype(I32)
    large = jnp.minimum(large, nb - 1)
    return ret + jnp.where(n < max_exact, n, large)


def _bias_diagonals(rel_bias, tq, tk):
    dstep = math.gcd(tq, tk)
    reach = (tq + tk + REL_MAX_DIST) // dstep
    band = [d for d in range(-reach * dstep, reach * dstep + 1, dstep)
            if d + tk - 1 > -REL_MAX_DIST and d - (tq - 1) < REL_MAX_DIST]
    far = tq + tk + REL_MAX_DIST
    ds = jnp.asarray([-far] + band + [far], I32)
    span = tq + tk
    m = jnp.arange(span, dtype=I32)
    q_minus_k = jnp.where(m < tq, m, m - span)
    bucket = _t5_bucket(ds[:, None] - q_minus_k[None, :])
    table = rel_bias.astype(F32)
    table = table - jnp.max(table, axis=0, keepdims=True)
    diag = jnp.moveaxis(table[bucket], -1, 0)
    return diag[:, :, None, :], band[0], dstep


def _attn_b(kmax, qb, kb, vbt, diag, lam_params, gsub, *, bsz, n, tq, tk, d0, dstep, fixed_shift):
    ntok = qb.shape[0]
    nq = n // tq
    return pl.pallas_call(
        functools.partial(_attn_b_kernel, tk=tk, d0=d0, dstep=dstep, fixed_shift=fixed_shift),
        grid=(bsz, B_HEADS, nq),
        in_specs=[
            pl.BlockSpec((1, LANES), lambda b, h, i: (0, 0)),
            pl.BlockSpec((tq, LANES), lambda b, h, i: (b * nq + i, h)),
            pl.BlockSpec((1, 2, n, LANES), lambda b, h, i: (h, 0, b, 0)),
            pl.BlockSpec((1, vbt.shape[1], n), lambda b, h, i: (h, 0, b)),
            pl.BlockSpec((1,) + diag.shape[1:], lambda b, h, i: (h, 0, 0, 0)),
            pl.BlockSpec(lam_params.shape, lambda b, h, i: (0, 0)),
            pl.BlockSpec((1, LANES), lambda b, h, i: (0, 0)),
        ],
        out_specs=pl.BlockSpec((tq, LANES), lambda b, h, i: (b * nq + i, h)),
        out_shape=jax.ShapeDtypeStruct((ntok, B_V_COLS), BF16),
        scratch_shapes=[pltpu.VMEM((diag.shape[1], tk, tq), F32)] + _score_scratch(fixed_shift, 2, tq, tk),
        compiler_params=_cparams(("arbitrary", "arbitrary", "arbitrary")),
        name="attn_b_shift" if fixed_shift else "attn_b_online",
    )(kmax, qb, kb, vbt, diag, lam_params, gsub)


def _outproj_kernel(x_ref, a_ref, b_ref, wo_ref, gffn_ref, wr_ref,
                    x1_ref, h2_ref, afft_ref, *, n_experts):
    wa = wo_ref.shape[0] // 2
    lane = lax.broadcasted_iota(I32, (1, LANES), 1)
    for rows in _row_chunks(x_ref.shape[0], OUTPROJ_ROW_CHUNKS):
        x1 = x_ref[rows, :] + _dot(a_ref[rows, :], wo_ref[:wa]) + _dot(b_ref[rows, :], wo_ref[wa:])
        x1_ref[rows, :] = x1
        h2 = x1 * lax.rsqrt(jnp.mean(x1 * x1, axis=-1, keepdims=True) + NORM_EPS) * gffn_ref[...]
        hi = h2.astype(BF16)
        h2_ref[rows, :] = hi
        lo = (h2 - hi.astype(F32)).astype(BF16)
        both = _dot(hi, wr_ref[...])
        logits = both[:, :LANES] + both[:, LANES:] + _dot(lo, wr_ref[:, :LANES])
        logits = jnp.where(lane < n_experts, logits, NEG_BIG)
        ex = jnp.exp(logits - jnp.max(logits, axis=-1, keepdims=True))
        aff = ex / jnp.sum(ex, axis=-1, keepdims=True)
        afft_ref[:, rows] = aff.T[:n_experts]


def _row_chunks(n_rows, n_chunks):
    chunk = max(n_rows // n_chunks, LANES)
    return [slice(r, r + chunk) for r in range(0, n_rows, chunk)]


def _outproj(x, a_out, b_out, w_out, gffn, wr, n_experts):
    ntok, d = x.shape
    tm = min(TOKEN_TILE, ntok)
    tok = lambda i: (i, 0)
    full = lambda i: (0, 0)
    return pl.pallas_call(
        functools.partial(_outproj_kernel, n_experts=n_experts),
        grid=(ntok // tm,),
        in_specs=[
            pl.BlockSpec((tm, d), tok),
            pl.BlockSpec((tm, a_out.shape[1]), tok),
            pl.BlockSpec((tm, b_out.shape[1]), tok),
            pl.BlockSpec(w_out.shape, full),
            pl.BlockSpec((1, d), full),
            pl.BlockSpec(wr.shape, full),
        ],
        out_specs=[
            pl.BlockSpec((tm, d), tok),
            pl.BlockSpec((tm, d), tok),
            pl.BlockSpec((n_experts, tm), lambda i: (0, i)),
        ],
        out_shape=[
            jax.ShapeDtypeStruct((ntok, d), F32),
            jax.ShapeDtypeStruct((ntok, d), BF16),
            jax.ShapeDtypeStruct((n_experts, ntok), F32),
        ],
        compiler_params=_cparams(("arbitrary",)),
        name="outproj",
    )(x, a_out, b_out, w_out, gffn, wr)


def _topk_kernel(aff_ref, slot_ref, gsel_ref, cstart_ref, *, cap):
    aff = aff_ref[...]
    n_exp, ntok = aff.shape

    def count(mask):
        return jnp.sum(mask.astype(I32), axis=1, keepdims=True)

    def as_float(bits):
        return lax.bitcast_convert_type(bits, np.float32)

    def value_bit(i, t):
        cand = t | lax.shift_left(jnp.int32(1), 30 - i)
        return jnp.where(count(aff >= as_float(cand)) >= cap, cand, t)

    thr = lax.fori_loop(0, 31, value_bit, jnp.zeros((n_exp, 1), I32))
    above = aff >= as_float(thr + 1)
    tie = (aff >= as_float(thr)) & jnp.logical_not(above)
    need = cap - count(above)
    idx = lax.broadcasted_iota(I32, aff.shape, 1)

    n_bits = int(ntok).bit_length()

    def index_bit(i, bound):
        cand = bound + lax.shift_left(jnp.int32(1), n_bits - 1 - i)
        ok = (cand <= ntok) & (count(tie & (idx < cand)) <= need)
        return jnp.where(ok, cand, bound)

    bound = lax.fori_loop(0, n_bits, index_bit, jnp.zeros((n_exp, 1), I32))
    sel = above | (tie & (idx < bound))
    gsel_ref[...] = jnp.where(sel, aff, 0.0)

    r = lax.broadcasted_iota(I32, (SUB, SUB), 0)
    c = lax.broadcasted_iota(I32, (SUB, SUB), 1)
    before = jnp.where(r < c, 1.0, 0.0).astype(BF16)
    carry = jnp.zeros((n_exp, 1), F32)
    n_sub = ntok // SUB
    for s in range(n_sub):
        sc = sel[:, s * SUB:(s + 1) * SUB]
        scf = jnp.where(sc, 1.0, 0.0)
        pos = (_dot(scf.astype(BF16), before) + carry).astype(I32)
        slot_ref[:, s * SUB:(s + 1) * SUB] = jnp.where(sc, pos, -1)
        cstart_ref[:, s:s + 1] = carry.astype(I32)
        carry = carry + jnp.sum(scf, axis=1, keepdims=True)


def _topk(afft, cap):
    n_exp, ntok = afft.shape
    nsub = ntok // SUB
    return pl.pallas_call(
        functools.partial(_topk_kernel, cap=cap),
        out_shape=[
            jax.ShapeDtypeStruct((n_exp, ntok), I32),
            jax.ShapeDtypeStruct((n_exp, ntok), F32),
            jax.ShapeDtypeStruct((n_exp, nsub), I32),
        ],
        compiler_params=pltpu.CompilerParams(vmem_limit_bytes=VMEM_LIMIT_BYTES),
        name="topk",
    )(afft)


def _gather_kernel(cstart_ref, dense_ref, slot_ref, gsel_ref, h_ref, xe_ref, gs_ref):
    i = pl.program_id(1)
    n_sub = h_ref.shape[0] // SUB
    per_step = xe_ref.shape[0]

    @pl.when(i == 0)
    def _():
        xe_ref[...] = jnp.zeros_like(xe_ref)
        gs_ref[...] = jnp.zeros_like(gs_ref)

    def compact(x, win):
        e = pl.program_id(0) * per_step + x
        row = lax.broadcasted_iota(I32, (win, SUB), 0)
        for s in range(n_sub):
            cs = cstart_ref[e, i * n_sub + s]
            rows = pl.ds(pl.multiple_of(lax.div(cs, 16) * 16, 16), win)
            sl = slot_ref[x, :, s * SUB:(s + 1) * SUB]
            hit = row == (sl - rows.start)
            onehot = jnp.where(hit, 1.0, 0.0).astype(BF16)
            moved = _dot(onehot, h_ref[s * SUB:(s + 1) * SUB, :])
            xe_ref[x, rows, :] = (xe_ref[x, rows, :].astype(F32) + moved).astype(BF16)
            g = gsel_ref[x, :, s * SUB:(s + 1) * SUB]
            gw = jnp.sum(jnp.where(hit, g, 0.0), axis=1, keepdims=True)
            gs_ref[x, rows, :] += jnp.broadcast_to(gw, (win, LANES))

    for x in range(per_step):
        dense = dense_ref[pl.program_id(0) * per_step + x, i]
        pl.when(dense == 0)(functools.partial(compact, x, GATHER_WIN_SPARSE))
        pl.when(dense != 0)(functools.partial(compact, x, GATHER_WIN))


def _gather(cstart, slot3, gsel3, h2, cap):
    n_exp = slot3.shape[0]
    ntok, d = h2.shape
    tb = min(GATHER_TB, ntok)
    ends = jnp.concatenate([cstart[:, 1:], jnp.full((n_exp, 1), cap, I32)], axis=1)
    most = jnp.max((ends - cstart).reshape(n_exp, ntok // tb, tb // SUB), axis=-1)
    dense = (most > GATHER_WIN_SPARSE - 16).astype(I32)
    per_step = math.gcd(GATHER_EXPERTS_PER_STEP, n_exp)
    rows = cap + GATHER_WIN
    return pl.pallas_call(
        _gather_kernel,
        grid_spec=pltpu.PrefetchScalarGridSpec(
            num_scalar_prefetch=2,
            grid=(n_exp // per_step, ntok // tb),
            in_specs=[
                pl.BlockSpec((per_step, 1, tb), lambda e, i, cs, dn: (e, 0, i)),
                pl.BlockSpec((per_step, 1, tb), lambda e, i, cs, dn: (e, 0, i)),
                pl.BlockSpec((tb, d), lambda e, i, cs, dn: (i, 0)),
            ],
            out_specs=[
                pl.BlockSpec((per_step, rows, d), lambda e, i, cs, dn: (e, 0, 0)),
                pl.BlockSpec((per_step, rows, LANES), lambda e, i, cs, dn: (e, 0, 0)),
            ],
        ),
        out_shape=[
            jax.ShapeDtypeStruct((n_exp, rows, d), BF16),
            jax.ShapeDtypeStruct((n_exp, rows, LANES), F32),
        ],
        compiler_params=_cparams(("arbitrary", "arbitrary")),
        name="gather",
    )(cstart, dense, slot3, gsel3, h2)


def _ffn_kernel(xe_ref, gs_ref, wg_ref, wu_ref, wd_ref, ye_ref, acc_ref):
    f = pl.program_id(1)
    cap = xe_ref.shape[1]
    rows = min(FFN_ROWS, cap)

    @pl.when(f == 0)
    def _():
        acc_ref[...] = jnp.zeros_like(acc_ref)

    wg = wg_ref[0].astype(BF16)
    wu = wu_ref[0].astype(BF16)
    wd = wd_ref[0].astype(BF16)
    for c in range(cap // rows):
        x = xe_ref[0, c * rows:(c + 1) * rows, :]
        g = _dot(x, wg)
        u = _dot(x, wu)
        hid = (g * jax.nn.sigmoid(g) * u).astype(BF16)
        acc_ref[c * rows:(c + 1) * rows, :] += _dot(hid, wd)

    @pl.when(f == pl.num_programs(1) - 1)
    def _():
        ye_ref[0] = (acc_ref[...] * gs_ref[0][:, :1]).astype(BF16)


def _ffn(xe, gs, w_gate, w_up, w_down, cap):
    n_exp, _, d = xe.shape
    ff = w_gate.shape[-1]
    tf = min(FFN_TF, ff)
    return pl.pallas_call(
        _ffn_kernel,
        grid=(n_exp, ff // tf),
        in_specs=[
            pl.BlockSpec((1, cap, d), lambda e, f: (e, 0, 0)),
            pl.BlockSpec((1, cap, LANES), lambda e, f: (e, 0, 0)),
            pl.BlockSpec((1, d, tf), lambda e, f: (e, 0, f)),
            pl.BlockSpec((1, d, tf), lambda e, f: (e, 0, f)),
            pl.BlockSpec((1, tf, d), lambda e, f: (e, f, 0)),
        ],
        out_specs=pl.BlockSpec((1, cap, d), lambda e, f: (e, 0, 0)),
        out_shape=jax.ShapeDtypeStruct((n_exp, cap, d), BF16),
        scratch_shapes=[pltpu.VMEM((cap, d), F32)],
        compiler_params=_cparams(("arbitrary", "arbitrary")),
        name="ffn",
    )(xe, gs, w_gate, w_up, w_down)


def _window_start(first_slot, cap, rows):
    return jnp.minimum(lax.div(first_slot, 16), (cap - rows) // 16) * 16


def _combine_kernel(cstart_ref, x1_ref, slot_ref, *refs, cap):
    *ye_refs, gfin_ref, y_ref = refs
    i = pl.program_id(0)
    n_sub = x1_ref.shape[0] // SUB
    rows = ye_refs[0].shape[1]
    win = min(COMBINE_WIN, rows)
    row = lax.broadcasted_iota(I32, (win, SUB), 0)
    dims = (((0,), (0,)), ((), ()))
    starts = [_window_start(cstart_ref[e, i * n_sub], cap, rows) for e in range(len(ye_refs))]
    for s in range(n_sub):
        out = x1_ref[s * SUB:(s + 1) * SUB, :]
        for e, ye_ref in enumerate(ye_refs):
            cs16 = lax.div(cstart_ref[e, i * n_sub + s], 16) * 16
            w0 = pl.multiple_of(jnp.clip(cs16 - starts[e], 0, rows - win), 16)
            sl = slot_ref[e, :, s * SUB:(s + 1) * SUB]
            onehot = jnp.where(row == (sl - (starts[e] + w0)), 1.0, 0.0).astype(BF16)
            out = out + lax.dot_general(onehot, ye_ref[0, pl.ds(w0, win), :], dims,
                                        preferred_element_type=F32)
        y_ref[s * SUB:(s + 1) * SUB, :] = (
            out * lax.rsqrt(jnp.mean(out * out, axis=-1, keepdims=True) + NORM_EPS) * gfin_ref[...])


def _combine_call(cstart, x1, slot3, ye, gfin, *, tb, rows):
    n_exp, cap, d = ye.shape
    ntok = x1.shape[0]
    n_sub = tb // SUB

    def window(e):
        return pl.BlockSpec((pl.Element(1), pl.Element(rows), pl.Element(d)),
                            lambda i, cs: (e, _window_start(cs[e, i * n_sub], cap, rows), 0))

    return pl.pallas_call(
        functools.partial(_combine_kernel, cap=cap),
        grid_spec=pltpu.PrefetchScalarGridSpec(
            num_scalar_prefetch=1,
            grid=(ntok // tb,),
            in_specs=[pl.BlockSpec((tb, d), lambda i, cs: (i, 0)),
                      pl.BlockSpec((n_exp, 1, tb), lambda i, cs: (0, 0, i))]
                     + [window(e) for e in range(n_exp)]
                     + [pl.BlockSpec((1, d), lambda i, cs: (0, 0))],
            out_specs=pl.BlockSpec((tb, d), lambda i, cs: (i, 0)),
        ),
        out_shape=jax.ShapeDtypeStruct((ntok, d), F32),
        compiler_params=_cparams(("arbitrary",)),
        name="combine",
    )(cstart, x1, slot3, *([ye] * n_exp), gfin)


def _combine(cstart, x1, slot3, ye, gfin):
    n_exp, cap, d = ye.shape
    ntok = x1.shape[0]
    tb = min(COMBINE_TB, ntok, cap // 2)
    n_sub = tb // SUB
    rows_dense = tb + 16
    rows_sparse = min(COMBINE_ROWS_SPARSE, rows_dense)
    assert rows_sparse >= SUB + 16 and rows_dense <= cap
    first = cstart[:, ::n_sub]
    used = jnp.concatenate([first[:, 1:], jnp.full((n_exp, 1), cap, I32)], axis=1) - first
    call = functools.partial(_combine_call, tb=tb)
    return lax.cond(jnp.max(used) <= rows_sparse - 16,
                    functools.partial(call, rows=rows_sparse),
                    functools.partial(call, rows=rows_dense),
                    cstart, x1, slot3, ye, gfin)


def _rope_tables(n):
    rows = n // GRID_W
    row = jnp.broadcast_to(jnp.arange(rows)[:, None], (rows, GRID_W)).reshape(-1).astype(F32)
    col = jnp.broadcast_to(jnp.arange(GRID_W)[None, :], (rows, GRID_W)).reshape(-1).astype(F32)
    half = HEAD_DIM // 2
    inv = ROPE_THETA ** (-jnp.arange(0, half, 2, dtype=F32) / half)
    ang_r = row[:, None] * inv[None, :]
    ang_c = col[:, None] * inv[None, :]
    cos = jnp.concatenate([jnp.cos(ang_r), jnp.cos(ang_r), jnp.cos(ang_c), jnp.cos(ang_c)], axis=-1)
    sin = jnp.concatenate([-jnp.sin(ang_r), jnp.sin(ang_r), -jnp.sin(ang_c), jnp.sin(ang_c)], axis=-1)
    return jnp.tile(cos, (1, 2)), jnp.tile(sin, (1, 2))


def _encoder(x, p):
    bsz, n, d = x.shape
    ntok = bsz * n
    n_exp = p["n_exp"]
    cap = CAPACITY_FACTOR * ntok // n_exp
    xt = x.reshape(ntok, d)
    cos, sin = _rope_tables(n)
    qa, ka, vat, qb, kb, vbt, stats = _proj(xt, p["gmix"], p["w_in"], p["gq"], p["gk"], cos, sin,
                                            p["seg"], n)

    qa_n, ka_n, qb_n, kb_n = jnp.sqrt(jnp.max(stats[:, :4, 0], axis=0)) * NORM_SLACK
    lanes = lambda v: jnp.broadcast_to(v, (1, LANES)).astype(F32)
    a_args = (lanes(ka_n), qa, ka, vat)
    a_out = lax.cond(qa_n * ka_n <= SAFE_RANGE,
                     functools.partial(_attn_a, bsz=bsz, n=n, fixed_shift=True),
                     functools.partial(_attn_a, bsz=bsz, n=n, fixed_shift=False), *a_args)
    tq, tk = min(ATTN_TQ, n), min(ATTN_TK, n)
    diag, d0, dstep = _bias_diagonals(p["rel_bias"], tq, tk)
    b_args = (lanes(kb_n), qb, kb, vbt, diag, p["lam"], p["gsub"])
    b_kw = dict(bsz=bsz, n=n, tq=tq, tk=tk, d0=d0, dstep=dstep)
    b_out = lax.cond(2.0 * qb_n * kb_n + p["bias_spread"] <= 2.0 * SAFE_RANGE,
                     functools.partial(_attn_b, fixed_shift=True, **b_kw),
                     functools.partial(_attn_b, fixed_shift=False, **b_kw), *b_args)

    x1, h2, afft = _outproj(xt, a_out, b_out, p["w_out"], p["gffn"], p["wr"], n_exp)
    slot, gsel, cstart = _topk(afft, cap)
    slot3 = slot.reshape(n_exp, 1, ntok)
    gsel3 = gsel.reshape(n_exp, 1, ntok)
    xe, gs = _gather(cstart, slot3, gsel3, h2, cap)
    ye = _ffn(xe, gs, p["w_gate"], p["w_up"], p["w_down"], cap)
    y = _combine(cstart, x1, slot3, ye, p["gfin"])
    return y.reshape(bsz, n, d)


def kernel(x_prompt, x_sample, norm_mix, w_in, a_q_norm, a_k_norm, b_lambda, b_subln, w_out, rel_bias,
           norm_ffn, w_router, w_gate, w_up, w_down, norm_final):
    d = x_prompt.shape[-1]
    n_exp = w_router.shape[-1]
    wr = jnp.zeros((d, LANES), F32).at[:, :n_exp].set(w_router[0].astype(F32))
    wr_hi = wr.astype(BF16)
    lane_head = np.arange(LANES) // HEAD_DIM
    p = dict(
        n_exp=n_exp,
        gmix=norm_mix[0].astype(F32).reshape(1, d),
        w_in=w_in[0].astype(BF16),
        gq=jnp.tile(a_q_norm[0].astype(F32), 2).reshape(1, LANES),
        gk=jnp.tile(a_k_norm[0].astype(F32), 2).reshape(1, LANES),
        seg=jnp.asarray(lane_head[:, None] == lane_head[None, :], BF16),
        rel_bias=rel_bias,
        bias_spread=(jnp.max(rel_bias) - jnp.min(rel_bias)).astype(F32),
        lam=b_lambda[0].astype(F32),
        gsub=b_subln[0].astype(F32).reshape(1, LANES),
        w_out=w_out[0].astype(BF16),
        gffn=norm_ffn[0].astype(F32).reshape(1, d),
        wr=jnp.concatenate([wr_hi, (wr - wr_hi.astype(F32)).astype(BF16)], axis=1),
        w_gate=w_gate[0], w_up=w_up[0], w_down=w_down[0],
        gfin=norm_final.astype(F32).reshape(1, d),
    )
    return (_encoder(x_prompt, p), _encoder(x_sample, p))
```
